```python
import math
import jax, jax.numpy as jnp
from jax import lax
import numpy as np

D_MODEL = 2048
BATCH = 1
SEQ = 8192
DEPTH = 2

N_A_LAYERS = DEPTH // 2
N_B_LAYERS = DEPTH - N_A_LAYERS

A_HEAD_DIM = 64
A_HEADS = D_MODEL // A_HEAD_DIM
A_KV_HEADS = A_HEADS // 8
A_WINDOW = 128
A_BLOCK = 128

B_HEAD_DIM = 128
B_HEADS = D_MODEL // B_HEAD_DIM
B_KV_HEADS = B_HEADS // 4
MOBA_BLOCK = 256
MOBA_TOPK = 3
MOBA_QCHUNK = 16

D_FF = 4 * D_MODEL

ROPE_THETA = 10000.0
LN_EPS = 1e-5
DEEPNORM_ALPHA = (2 * DEPTH) ** 0.25
DEEPNORM_BETA = (8 * DEPTH) ** -0.25

kernel_name = "yoco_swa_sink_moba_hybrid"


def layer_norm(x, g, b):
    xf = x.astype(jnp.float32)
    mu = jnp.mean(xf, axis=-1, keepdims=True)
    var = jnp.mean(jnp.square(xf - mu), axis=-1, keepdims=True)
    y = (xf - mu) * lax.rsqrt(var + LN_EPS)
    return (y * g.astype(jnp.float32) + b.astype(jnp.float32)).astype(x.dtype)


def rope(x, positions):
    d = x.shape[-1]
    half = d // 2
    inv_freq = ROPE_THETA ** (-(jnp.arange(half, dtype=jnp.float32) * 2.0) / d)
    ang = positions.astype(jnp.float32)[..., None] * inv_freq
    cos = jnp.cos(ang)[:, :, None, :]
    sin = jnp.sin(ang)[:, :, None, :]
    x1 = x[..., :half].astype(jnp.float32)
    x2 = x[..., half:].astype(jnp.float32)
    out = jnp.concatenate([x1 * cos - x2 * sin, x2 * cos + x1 * sin], axis=-1)
    return out.astype(x.dtype)


def mixer_a(h, positions, w_qkv, b_qkv, sinks, w_o):
    B, S, _ = h.shape
    G = A_HEADS // A_KV_HEADS
    nq = A_HEADS * A_HEAD_DIM
    nk = A_KV_HEADS * A_HEAD_DIM
    qkv = h @ w_qkv + b_qkv
    q = rope(qkv[..., :nq].reshape(B, S, A_HEADS, A_HEAD_DIM), positions)
    k = rope(qkv[..., nq:nq + nk].reshape(B, S, A_KV_HEADS, A_HEAD_DIM), positions)
    v = qkv[..., nq + nk:].reshape(B, S, A_KV_HEADS, A_HEAD_DIM)

    nb = S // A_BLOCK
    qb = q.reshape(B, nb, A_BLOCK, A_KV_HEADS, G, A_HEAD_DIM)
    pad = ((0, 0), (A_BLOCK, 0), (0, 0), (0, 0))
    kp = jnp.pad(k, pad)
    vp = jnp.pad(v, pad)
    kband = jnp.concatenate([kp[:, :S].reshape(B, nb, A_BLOCK, A_KV_HEADS, A_HEAD_DIM),
                             k.reshape(B, nb, A_BLOCK, A_KV_HEADS, A_HEAD_DIM)], axis=2)
    vband = jnp.concatenate([vp[:, :S].reshape(B, nb, A_BLOCK, A_KV_HEADS, A_HEAD_DIM),
                             v.reshape(B, nb, A_BLOCK, A_KV_HEADS, A_HEAD_DIM)], axis=2)

    s = jnp.einsum('bnqkgd,bnjkd->bnkgqj', qb, kband).astype(jnp.float32) * (A_HEAD_DIM ** -0.5)
    qi = jnp.arange(A_BLOCK)[:, None]
    kj = jnp.arange(2 * A_BLOCK)[None, :]
    rel = qi + A_BLOCK - kj
    in_win = (rel >= 0) & (rel < A_WINDOW)
    kpos = jnp.arange(nb)[:, None] * A_BLOCK - A_BLOCK + jnp.arange(2 * A_BLOCK)[None, :]
    mask = in_win[None] & (kpos >= 0)[:, None, :]
    s = jnp.where(mask[None, :, None, None], s, -jnp.inf)
    sink = jnp.broadcast_to(
        sinks.astype(jnp.float32).reshape(A_KV_HEADS, G)[None, None, :, :, None, None],
        s.shape[:-1] + (1,))
    p = jax.nn.softmax(jnp.concatenate([s, sink], axis=-1), axis=-1)[..., :-1]
    p = p.astype(v.dtype)
    o = jnp.einsum('bnkgqj,bnjkd->bnqkgd', p, vband).reshape(B, S, A_HEADS * A_HEAD_DIM)
    return o @ w_o


def moba_shared_kv(h, positions, w_kv):
    B, S, _ = h.shape
    nk = B_KV_HEADS * B_HEAD_DIM
    kv = h @ w_kv
    k = rope(kv[..., :nk].reshape(B, S, B_KV_HEADS, B_HEAD_DIM), positions)
    v = kv[..., nk:].reshape(B, S, B_KV_HEADS, B_HEAD_DIM)
    Sp = -(-S // MOBA_BLOCK) * MOBA_BLOCK
    pad = ((0, 0), (0, Sp - S), (0, 0), (0, 0))
    k = jnp.pad(k, pad)
    v = jnp.pad(v, pad)
    nb = Sp // MOBA_BLOCK
    kb = k.reshape(B, nb, MOBA_BLOCK, B_KV_HEADS, B_HEAD_DIM).transpose(0, 3, 1, 2, 4)
    vb = v.reshape(B, nb, MOBA_BLOCK, B_KV_HEADS, B_HEAD_DIM).transpose(0, 3, 1, 2, 4)
    kbar = jnp.mean(kb.astype(jnp.float32), axis=3).astype(k.dtype)
    return kb, vb, kbar


def mixer_b(h, positions, w_q, w_o, kb, vb, kbar):
    B, S, _ = h.shape
    nb = kb.shape[2]
    Sp = nb * MOBA_BLOCK
    G = B_HEADS // B_KV_HEADS
    q = rope((h @ w_q).reshape(B, S, B_HEADS, B_HEAD_DIM), positions)
    q = jnp.pad(q, ((0, 0), (0, Sp - S), (0, 0), (0, 0)))

    g = jnp.einsum('bskgd,bknd->bskgn', q.reshape(B, Sp, B_KV_HEADS, G, B_HEAD_DIM), kbar)
    g = g.astype(jnp.float32).reshape(B, Sp, B_HEADS, nb)
    qpos = jnp.arange(Sp)
    qblk = qpos // MOBA_BLOCK
    past = jnp.arange(nb)[None, :] < qblk[:, None]
    g = jnp.where(past[None, :, None, :], g, -jnp.inf)
    n_sel = min(MOBA_TOPK, nb)
    _, top_idx = lax.top_k(g, n_sel)
    own = jnp.broadcast_to(qblk[None, :, None, None], (B, Sp, B_HEADS, 1)).astype(top_idx.dtype)
    sel = jnp.concatenate([top_idx, own], axis=-1)
    NS = n_sel + 1
    is_own = jnp.arange(NS) == NS - 1

    nchunk = Sp // MOBA_QCHUNK
    q_ch = q.reshape(B, nchunk, MOBA_QCHUNK, B_HEADS, B_HEAD_DIM).swapaxes(0, 1)
    sel_ch = sel.reshape(B, nchunk, MOBA_QCHUNK, B_HEADS, NS).swapaxes(0, 1)
    pos_ch = qpos.reshape(nchunk, MOBA_QCHUNK)
    b_ix = jnp.arange(B)[:, None, None, None]
    kvh_ix = (jnp.arange(B_HEADS) // G)[None, None, :, None]
    scale = B_HEAD_DIM ** -0.5

    def attend(args):
        qc, sc, pc = args
        kg = kb[b_ix, kvh_ix, sc]
        vg = vb[b_ix, kvh_ix, sc]
        s = jnp.einsum('bqhd,bqhnjd->bqhnj', qc, kg).astype(jnp.float32) * scale
        kpos = sc[..., None] * MOBA_BLOCK + jnp.arange(MOBA_BLOCK)
        qp = pc[None, :, None, None, None]
        cb = (pc // MOBA_BLOCK)[None, :, None, None, None]
        valid = jnp.where(is_own[:, None], kpos <= qp, sc[..., None] < cb)
        s = jnp.where(valid, s, -jnp.inf)
        p = jax.nn.softmax(s.reshape(s.shape[:3] + (NS * MOBA_BLOCK,)), axis=-1)
        p = p.reshape(s.shape).astype(vg.dtype)
        return jnp.einsum('bqhnj,bqhnjd->bqhd', p, vg)

    o = lax.map(attend, (q_ch, sel_ch, pos_ch))
    o = o.swapaxes(0, 1).reshape(B, Sp, B_HEADS * B_HEAD_DIM)[:, :S]
    return o @ w_o


def sq_relu_mlp(h, w_in, w_out):
    return jnp.square(jax.nn.relu(h @ w_in)) @ w_out


def setup_inputs(seed: int = 0) -> dict:
    key = jax.random.key(seed)
    ks = jax.random.split(key, 20)

    def nrm(k, shape, fan_in, gain=1.0):
        return jax.random.normal(k, shape, jnp.float32) * (gain * fan_in ** -0.5)

    x = jax.random.normal(ks[0], (BATCH, SEQ, D_MODEL), jnp.float32)
    positions = jnp.broadcast_to(jnp.arange(SEQ, dtype=jnp.int32), (BATCH, SEQ))

    nq = A_HEADS * A_HEAD_DIM
    nk = A_KV_HEADS * A_HEAD_DIM
    a_wq = nrm(ks[1], (N_A_LAYERS, D_MODEL, nq), D_MODEL)
    a_wk = nrm(ks[2], (N_A_LAYERS, D_MODEL, nk), D_MODEL)
    a_wv = nrm(ks[3], (N_A_LAYERS, D_MODEL, nk), D_MODEL, DEEPNORM_BETA)
    a_w_qkv = jnp.concatenate([a_wq, a_wk, a_wv], axis=-1)
    a_b_qkv = 0.02 * jax.random.normal(ks[4], (N_A_LAYERS, nq + 2 * nk), jnp.float32)
    a_sinks = jax.random.normal(ks[5], (N_A_LAYERS, A_HEADS), jnp.float32)
    a_w_o = nrm(ks[6], (N_A_LAYERS, nq, D_MODEL), nq, DEEPNORM_BETA)

    bq = B_HEADS * B_HEAD_DIM
    bk = B_KV_HEADS * B_HEAD_DIM
    b_w_q = nrm(ks[7], (N_B_LAYERS, D_MODEL, bq), D_MODEL)
    b_w_o = nrm(ks[8], (N_B_LAYERS, bq, D_MODEL), bq, DEEPNORM_BETA)
    b_w_kv = jnp.concatenate([nrm(ks[9], (D_MODEL, bk), D_MODEL),
                              nrm(ks[10], (D_MODEL, bk), D_MODEL, DEEPNORM_BETA)], axis=-1)

    ln_attn_g = 1.0 + 0.02 * jax.random.normal(ks[11], (DEPTH, D_MODEL), jnp.float32)
    ln_attn_b = 0.02 * jax.random.normal(ks[12], (DEPTH, D_MODEL), jnp.float32)
    ln_mlp_g = 1.0 + 0.02 * jax.random.normal(ks[13], (DEPTH, D_MODEL), jnp.float32)
    ln_mlp_b = 0.02 * jax.random.normal(ks[14], (DEPTH, D_MODEL), jnp.float32)
    mlp_w_in = nrm(ks[15], (DEPTH, D_MODEL, D_FF), D_MODEL)
    mlp_w_out = nrm(ks[16], (DEPTH, D_FF, D_MODEL), D_FF, DEEPNORM_BETA)

    return {"x": x, "positions": positions,
            "a_w_qkv": a_w_qkv, "a_b_qkv": a_b_qkv, "a_sinks": a_sinks, "a_w_o": a_w_o,
            "b_w_q": b_w_q, "b_w_o": b_w_o, "b_w_kv": b_w_kv,
            "ln_attn_g": ln_attn_g, "ln_attn_b": ln_attn_b,
            "ln_mlp_g": ln_mlp_g, "ln_mlp_b": ln_mlp_b,
            "mlp_w_in": mlp_w_in, "mlp_w_out": mlp_w_out}


def reference(x, positions, a_w_qkv, a_b_qkv, a_sinks, a_w_o, b_w_q, b_w_o, b_w_kv,
              ln_attn_g, ln_attn_b, ln_mlp_g, ln_mlp_b, mlp_w_in, mlp_w_out):
    h = x
    kb = vb = kbar = None
    for layer in range(DEPTH):
        if layer < N_A_LAYERS:
            mix = mixer_a(h, positions, a_w_qkv[layer], a_b_qkv[layer], a_sinks[layer], a_w_o[layer])
        else:
            if layer == N_A_LAYERS:
                kb, vb, kbar = moba_shared_kv(h, positions, b_w_kv)
            bl = layer - N_A_LAYERS
            mix = mixer_b(h, positions, b_w_q[bl], b_w_o[bl], kb, vb, kbar)
        h = layer_norm(DEEPNORM_ALPHA * h + mix, ln_attn_g[layer], ln_attn_b[layer])
        h = layer_norm(DEEPNORM_ALPHA * h + sq_relu_mlp(h, mlp_w_in[layer], mlp_w_out[layer]),
                       ln_mlp_g[layer], ln_mlp_b[layer])
    return h
```

```python
import functools

import jax
import jax.numpy as jnp
from jax import lax
from jax.experimental import pallas as pl
from jax.experimental.pallas import tpu as pltpu

A_HEAD_DIM = 64
A_KV_HEADS = 4
A_WINDOW = 128
B_HEAD_DIM = 128
B_KV_HEADS = 4
MOBA_BLOCK = 256
MOBA_TOPK = 3
ROPE_THETA = 10000.0
LN_EPS = 1e-5
DEPTH = 2
DEEPNORM_ALPHA = (2 * DEPTH) ** 0.25

V7X_VMEM_BYTES = 64 * 1024 * 1024
LANES = 128

BF16 = jnp.bfloat16
F32 = jnp.float32
MASKED = -1e30

_NT = (((1,), (1,)), ((), ()))


def _resident(shape):
    return pl.BlockSpec(shape, lambda *_: (0,) * len(shape), pipeline_mode=pl.Buffered(1))


def _params(semantics, vmem_bytes):
    return pltpu.CompilerParams(dimension_semantics=semantics,
                                vmem_limit_bytes=min(int(vmem_bytes), V7X_VMEM_BYTES))


def _layer_norm(z, g, b):
    mu = jnp.mean(z, axis=-1, keepdims=True)
    zc = z - mu
    var = jnp.mean(zc * zc, axis=-1, keepdims=True)
    return zc * lax.rsqrt(var + LN_EPS) * g + b


def _rope_table_kernel(pos_ref, inv_ref, cos_ref, sin_ref):
    ang = pos_ref[...].astype(F32) * inv_ref[...]
    cos_ref[...] = jnp.cos(ang)
    sin_ref[...] = jnp.sin(ang)


def _rope_tables(pos_row, head_dim, ts=2048):
    seq = pos_row.shape[1]
    half = head_dim // 2
    inv_freq = ROPE_THETA ** (-(jnp.arange(half, dtype=F32) * 2.0) / head_dim)
    out = jax.ShapeDtypeStruct((half, seq), F32)
    return pl.pallas_call(
        _rope_table_kernel,
        grid=(seq // ts,),
        in_specs=[pl.BlockSpec((1, ts), lambda i: (0, i)),
                  pl.BlockSpec((half, 1), lambda i: (0, 0))],
        out_specs=[pl.BlockSpec((half, ts), lambda i: (0, i))] * 2,
        out_shape=[out, out],
        name="rope_tables",
    )(pos_row, inv_freq[:, None])


def _rope_rows(y, base, half, c, s):
    x1 = y[base:base + half]
    x2 = y[base + half:base + 2 * half]
    return x1 * c - x2 * s, x2 * c + x1 * s


def _proj_kernel(x_ref, w_ref, b_ref, cos_ref, sin_ref, qT_ref, kn_ref, vT_ref, *rest,
                 n_q_heads, n_kv_heads, head_dim, q_scale, kbar_block):
    half = head_dim // 2
    nq = n_q_heads * head_dim
    nk = n_kv_heads * head_dim
    xb = x_ref[...].astype(BF16)
    y = lax.dot_general(w_ref[...], xb, _NT, preferred_element_type=F32)
    y = y + b_ref[...]
    c = cos_ref[...]
    s = sin_ref[...]
    for h in range(n_q_heads):
        r1, r2 = _rope_rows(y, h * head_dim, half, c, s)
        qT_ref[h * head_dim:h * head_dim + half, :] = (r1 * q_scale).astype(BF16)
        qT_ref[h * head_dim + half:(h + 1) * head_dim, :] = (r2 * q_scale).astype(BF16)
    k_rows = []
    for g in range(n_kv_heads):
        k_rows.extend(_rope_rows(y, nq + g * head_dim, half, c, s))
    k_nat = jnp.concatenate(k_rows, axis=0).T
    kn_ref[...] = k_nat.astype(BF16)
    vT_ref[...] = y[nq + nk:nq + 2 * nk].astype(BF16)
    if kbar_block:
        (kbar_ref,) = rest
        tm = k_nat.shape[0]
        for blk in range(tm // kbar_block):
            kbar_ref[0, blk:blk + 1, :] = jnp.mean(
                k_nat[blk * kbar_block:(blk + 1) * kbar_block], axis=0, keepdims=True)


def _project(x, wT, b_col, cos, sin, *, n_q_heads, n_kv_heads, head_dim, q_scale,
             kbar_block=0, tm=512):
    seq, d_model = x.shape
    n_out = wT.shape[0]
    nq = n_q_heads * head_dim
    nk = n_kv_heads * head_dim
    half = head_dim // 2
    out_shape = [jax.ShapeDtypeStruct((nq, seq), BF16),
                 jax.ShapeDtypeStruct((seq, nk), BF16),
                 jax.ShapeDtypeStruct((nk, seq), BF16)]
    out_specs = [pl.BlockSpec((nq, tm), lambda i: (0, i)),
                 pl.BlockSpec((tm, nk), lambda i: (i, 0)),
                 pl.BlockSpec((nk, tm), lambda i: (0, i))]
    if kbar_block:
        nb = tm // kbar_block
        out_shape.append(jax.ShapeDtypeStruct((seq // tm, nb, nk), F32))
        out_specs.append(pl.BlockSpec((1, nb, nk), lambda i: (i, 0, 0)))
    vmem = (2 * tm * d_model * 4 + n_out * d_model * 2 + n_out * LANES * 4
            + 2 * n_out * tm * 2 + 4 * n_out * tm * 4)
    return pl.pallas_call(
        functools.partial(_proj_kernel, n_q_heads=n_q_heads, n_kv_heads=n_kv_heads,
                          head_dim=head_dim, q_scale=q_scale, kbar_block=kbar_block),
        grid=(seq // tm,),
        in_specs=[pl.BlockSpec((tm, d_model), lambda i: (i, 0)),
                  _resident((n_out, d_model)),
                  _resident((n_out, 1)),
                  pl.BlockSpec((half, tm), lambda i: (0, i)),
                  pl.BlockSpec((half, tm), lambda i: (0, i))],
        out_specs=out_specs,
        out_shape=out_shape,
        compiler_params=_params(("parallel",), vmem),
        name="qkv_proj_hd%d" % head_dim,
    )(x, wT, b_col, cos, sin)


def _swa_kernel(qT_ref, kprev_ref, kcur_ref, vprev_ref, vcur_ref, sink_ref, o_ref, *,
                n_kv_heads, group, head_dim, blk, n_sub):
    i = pl.program_id(0)
    width = group * blk
    kj = lax.broadcasted_iota(jnp.int32, (blk, width), 0)
    qi = lax.broadcasted_iota(jnp.int32, (blk, width), 1) & (blk - 1)
    in_prev = kj > qi
    in_cur = kj <= qi
    kv_dim = n_kv_heads * head_dim
    for c in range(n_sub):
        lo, hi = c * blk, (c + 1) * blk
        if c == 0:
            k_prev, has_prev = kprev_ref[...], i > 0
        else:
            k_prev, has_prev = kcur_ref[lo - blk:lo, :], True
        k_cur = kcur_ref[lo:hi, :]
        for g in range(n_kv_heads):
            heads = range(g * group, (g + 1) * group)
            q_g = jnp.concatenate(
                [qT_ref[h * head_dim:(h + 1) * head_dim, lo:hi] for h in heads], axis=1)
            pads = [jnp.zeros((g * head_dim, width), BF16)] if g else []
            pads.append(q_g)
            if g + 1 < n_kv_heads:
                pads.append(jnp.zeros((kv_dim - (g + 1) * head_dim, width), BF16))
            q_pad = jnp.concatenate(pads, axis=0) if len(pads) > 1 else q_g
            s_p = jnp.dot(k_prev, q_pad, preferred_element_type=F32)
            s_c = jnp.dot(k_cur, q_pad, preferred_element_type=F32)
            s_p = jnp.where(jnp.logical_and(in_prev, has_prev), s_p, MASKED)
            s_c = jnp.where(in_cur, s_c, MASKED)
            sink = sink_ref[:, g * width:(g + 1) * width]
            m = jnp.maximum(jnp.maximum(jnp.max(s_p, axis=0, keepdims=True),
                                        jnp.max(s_c, axis=0, keepdims=True)), sink)
            p_p = jnp.exp(s_p - m)
            p_c = jnp.exp(s_c - m)
            denom = (jnp.sum(p_p, axis=0, keepdims=True) + jnp.sum(p_c, axis=0, keepdims=True)
                     + jnp.exp(sink - m))
            rows = slice(g * head_dim, (g + 1) * head_dim)
            if c == 0:
                v_prev = vprev_ref[rows, :]
            else:
                v_prev = vcur_ref[rows, lo - blk:lo]
            v_cur = vcur_ref[rows, lo:hi]
            oT = (jnp.dot(v_prev, p_p.astype(BF16), preferred_element_type=F32)
                  + jnp.dot(v_cur, p_c.astype(BF16), preferred_element_type=F32))
            oT = oT / denom
            o_hd = jnp.concatenate([oT[:, h * blk:(h + 1) * blk] for h in range(group)], axis=0)
            o_ref[lo:hi, g * group * head_dim:(g + 1) * group * head_dim] = o_hd.T.astype(BF16)


def _swa_attention(qT, kn, vT, sink_row, *, n_kv_heads, head_dim, blk, tq=256):
    nq, seq = qT.shape
    kv_dim = n_kv_heads * head_dim
    group = nq // kv_dim
    n_sub = tq // blk
    prev_blk = lambda i: jnp.maximum(i * n_sub - 1, 0)
    vmem = 2 * (nq * tq * 2 * 2 + 2 * (tq + blk) * kv_dim * 2) + 16 * blk * group * blk * 4
    return pl.pallas_call(
        functools.partial(_swa_kernel, n_kv_heads=n_kv_heads, group=group,
                          head_dim=head_dim, blk=blk, n_sub=n_sub),
        grid=(seq // tq,),
        in_specs=[pl.BlockSpec((nq, tq), lambda i: (0, i)),
                  pl.BlockSpec((blk, kv_dim), lambda i: (prev_blk(i), 0)),
                  pl.BlockSpec((tq, kv_dim), lambda i: (i, 0)),
                  pl.BlockSpec((kv_dim, blk), lambda i: (0, prev_blk(i))),
                  pl.BlockSpec((kv_dim, tq), lambda i: (0, i)),
                  _resident(sink_row.shape)],
        out_specs=pl.BlockSpec((tq, nq), lambda i: (i, 0)),
        out_shape=jax.ShapeDtypeStruct((seq, nq), BF16),
        compiler_params=_params(("parallel",), vmem),
        name="swa_attention",
    )(qT, kn, kn, vT, vT, sink_row)


def _moba_kernel(qT_ref, kn_ref, vT_ref, kbar_ref, o_ref,
                 q_all_ref, bias_ref, m_ref, l_ref, acc_ref, *, group, head_dim, blk, topk):
    qt = pl.program_id(1)
    width = group * blk
    q_all_ref[...] = jnp.concatenate(
        [qT_ref[h * head_dim:(h + 1) * head_dim, :] for h in range(group)], axis=1)
    q_all = q_all_ref[...]

    nb = kbar_ref.shape[0]
    gate = jnp.dot(kbar_ref[...].astype(BF16), q_all, preferred_element_type=F32)
    row = lax.broadcasted_iota(jnp.int32, (nb, width), 0).astype(F32)
    past = row < qt.astype(F32)
    gate = jnp.where(past, gate, -jnp.inf)
    chosen = jnp.zeros((nb, width), F32)
    for _ in range(topk):
        best = jnp.max(gate, axis=0, keepdims=True)
        idx = jnp.min(jnp.where(gate == best, row, float(nb)), axis=0, keepdims=True)
        hit = row == idx
        chosen = jnp.where(hit, 1.0, chosen)
        gate = jnp.where(hit, -jnp.inf, gate)
    bias_ref[...] = jnp.where(jnp.logical_and(chosen > 0.0, past), 0.0, MASKED)

    own = pl.multiple_of(qt * blk, blk)
    s = jnp.dot(kn_ref[pl.ds(own, blk), :], q_all, preferred_element_type=F32)
    kj = lax.broadcasted_iota(jnp.int32, (blk, width), 0)
    qi = lax.broadcasted_iota(jnp.int32, (blk, width), 1) & (blk - 1)
    s = jnp.where(kj <= qi, s, MASKED)
    m = jnp.max(s, axis=0, keepdims=True)
    p = jnp.exp(s - m)
    m_ref[...] = m
    l_ref[...] = jnp.sum(p, axis=0, keepdims=True)
    acc_ref[...] = jnp.dot(vT_ref[:, pl.ds(own, blk)], p.astype(BF16), preferred_element_type=F32)

    def past_block(j, carry):
        start = pl.multiple_of(j * blk, blk)
        s = jnp.dot(kn_ref[pl.ds(start, blk), :], q_all_ref[...], preferred_element_type=F32)
        s = s + bias_ref[pl.ds(j, 1), :]
        m_old = m_ref[...]
        m_new = jnp.maximum(m_old, jnp.max(s, axis=0, keepdims=True))
        alpha = jnp.exp(m_old - m_new)
        p = jnp.exp(s - m_new)
        l_ref[...] = alpha * l_ref[...] + jnp.sum(p, axis=0, keepdims=True)
        acc_ref[...] = alpha * acc_ref[...] + jnp.dot(
            vT_ref[:, pl.ds(start, blk)], p.astype(BF16), preferred_element_type=F32)
        m_ref[...] = m_new
        return carry

    lax.fori_loop(0, qt, past_block, 0)

    oT = acc_ref[...] / l_ref[...]
    o_hd = jnp.concatenate([oT[:, h * blk:(h + 1) * blk] for h in range(group)], axis=0)
    o_ref[...] = o_hd.T.astype(BF16)


def _moba_attention(qT, kn, vT, kbar, *, n_kv_heads, head_dim, blk, topk):
    nq, seq = qT.shape
    group = nq // (n_kv_heads * head_dim)
    width = group * blk
    nb = seq // blk
    vmem = (4 * seq * head_dim * 2 + 4 * group * head_dim * blk * 2
            + head_dim * width * (2 + 4) + 8 * blk * width * 4)
    return pl.pallas_call(
        functools.partial(_moba_kernel, group=group, head_dim=head_dim, blk=blk, topk=topk),
        grid=(n_kv_heads, nb),
        in_specs=[pl.BlockSpec((group * head_dim, blk), lambda g, t: (g, t)),
                  pl.BlockSpec((seq, head_dim), lambda g, t: (0, g)),
                  pl.BlockSpec((head_dim, seq), lambda g, t: (g, 0)),
                  pl.BlockSpec((nb, head_dim), lambda g, t: (0, g))],
        out_specs=pl.BlockSpec((blk, group * head_dim), lambda g, t: (t, g)),
        out_shape=jax.ShapeDtypeStruct((seq, nq), BF16),
        scratch_shapes=[pltpu.VMEM((head_dim, width), BF16),
                        pltpu.VMEM((nb, width), F32),
                        pltpu.VMEM((1, width), F32),
                        pltpu.VMEM((1, width), F32),
                        pltpu.VMEM((head_dim, width), F32)],
        compiler_params=_params(("parallel", "parallel"), vmem),
        name="moba_attention",
    )(qT, kn, vT, kbar)


def _oproj_ln_kernel(o_ref, w_ref, x_ref, g_ref, b_ref, h_ref):
    y = jnp.dot(o_ref[...], w_ref[...], preferred_element_type=F32)
    h_ref[...] = _layer_norm(DEEPNORM_ALPHA * x_ref[...] + y, g_ref[...], b_ref[...])


def _oproj_ln(o, w, x, g_row, b_row, tm=512):
    seq, d_model = x.shape
    k_dim = o.shape[1]
    vmem = (2 * tm * k_dim * 2 + k_dim * d_model * 2 + 4 * tm * d_model * 4
            + 3 * tm * d_model * 4)
    return pl.pallas_call(
        _oproj_ln_kernel,
        grid=(seq // tm,),
        in_specs=[pl.BlockSpec((tm, k_dim), lambda i: (i, 0)),
                  _resident((k_dim, d_model)),
                  pl.BlockSpec((tm, d_model), lambda i: (i, 0)),
                  _resident((1, d_model)),
                  _resident((1, d_model))],
        out_specs=pl.BlockSpec((tm, d_model), lambda i: (i, 0)),
        out_shape=jax.ShapeDtypeStruct((seq, d_model), F32),
        compiler_params=_params(("parallel",), vmem),
        name="oproj_ln",
    )(o, w, x, g_row, b_row)


def _mlp_ln_kernel(h_ref, win_ref, wout_ref, g_ref, b_ref, out_ref, hb_ref, acc_ref):
    f = pl.program_id(1)

    @pl.when(f == 0)
    def _():
        hb_ref[...] = h_ref[...].astype(BF16)
        acc_ref[...] = jnp.zeros_like(acc_ref)

    a = jnp.dot(hb_ref[...], win_ref[...], preferred_element_type=F32)
    a = jnp.maximum(a, 0.0)
    a = a * a
    acc_ref[...] += jnp.dot(a.astype(BF16), wout_ref[...], preferred_element_type=F32)

    @pl.when(f == pl.num_programs(1) - 1)
    def _():
        out_ref[...] = _layer_norm(DEEPNORM_ALPHA * h_ref[...] + acc_ref[...],
                                   g_ref[...], b_ref[...])


def _mlp_ln(h, w_in, w_out, g_row, b_row, tm=512, tf=1024):
    seq, d_model = h.shape
    d_ff = w_in.shape[1]
    vmem = (4 * tm * d_model * 4 + 4 * d_model * tf * 2 + tm * d_model * (2 + 4)
            + 2 * tm * tf * 4 + 2 * tm * d_model * 4)
    return pl.pallas_call(
        _mlp_ln_kernel,
        grid=(seq // tm, d_ff // tf),
        in_specs=[pl.BlockSpec((tm, d_model), lambda i, f: (i, 0)),
                  pl.BlockSpec((d_model, tf), lambda i, f: (0, f)),
                  pl.BlockSpec((tf, d_model), lambda i, f: (f, 0)),
                  _resident((1, d_model)),
                  _resident((1, d_model))],
        out_specs=pl.BlockSpec((tm, d_model), lambda i, f: (i, 0)),
        out_shape=jax.ShapeDtypeStruct((seq, d_model), F32),
        scratch_shapes=[pltpu.VMEM((tm, d_model), BF16), pltpu.VMEM((tm, d_model), F32)],
        compiler_params=_params(("parallel", "arbitrary"), vmem),
        name="mlp_ln",
    )(h, w_in, w_out, g_row, b_row)


def kernel(x, positions, a_w_qkv, a_b_qkv, a_sinks, a_w_o, b_w_q, b_w_o, b_w_kv,
           ln_attn_g, ln_attn_b, ln_mlp_g, ln_mlp_b, mlp_w_in, mlp_w_out):
    batch, seq, d_model = x.shape
    assert batch == 1 and a_w_qkv.shape[0] == 1 and b_w_q.shape[0] == 1
    h = x.reshape(seq, d_model)
    pos_row = positions.reshape(1, seq)
    row = lambda v: v.reshape(1, -1)

    cos_a, sin_a = _rope_tables(pos_row, A_HEAD_DIM)
    n_a_heads = a_sinks.shape[1]
    qT, kn, vT = _project(
        h, a_w_qkv[0].T.astype(BF16), a_b_qkv[0][:, None], cos_a, sin_a,
        n_q_heads=n_a_heads, n_kv_heads=A_KV_HEADS, head_dim=A_HEAD_DIM,
        q_scale=A_HEAD_DIM ** -0.5)
    sink_row = jnp.repeat(a_sinks[0].astype(F32), A_WINDOW)[None, :]
    o = _swa_attention(qT, kn, vT, sink_row, n_kv_heads=A_KV_HEADS, head_dim=A_HEAD_DIM,
                       blk=A_WINDOW)
    h = _oproj_ln(o, a_w_o[0].astype(BF16), h, row(ln_attn_g[0]), row(ln_attn_b[0]))
    h = _mlp_ln(h, mlp_w_in[0].astype(BF16), mlp_w_out[0].astype(BF16),
                row(ln_mlp_g[0]), row(ln_mlp_b[0]))

    cos_b, sin_b = _rope_tables(pos_row, B_HEAD_DIM)
    w1T = jnp.concatenate([b_w_q[0].T, b_w_kv.T], axis=0).astype(BF16)
    n_b_heads = b_w_q.shape[2] // B_HEAD_DIM
    qT, kn, vT, kbar = _project(
        h, w1T, jnp.zeros((w1T.shape[0], 1), F32), cos_b, sin_b,
        n_q_heads=n_b_heads, n_kv_heads=B_KV_HEADS, head_dim=B_HEAD_DIM,
        q_scale=B_HEAD_DIM ** -0.5, kbar_block=MOBA_BLOCK)
    kbar = kbar.reshape(seq // MOBA_BLOCK, B_KV_HEADS * B_HEAD_DIM)
    o = _moba_attention(qT, kn, vT, kbar, n_kv_heads=B_KV_HEADS, head_dim=B_HEAD_DIM,
                        blk=MOBA_BLOCK, topk=MOBA_TOPK)
    h = _oproj_ln(o, b_w_o[0].astype(BF16), h, row(ln_attn_g[1]), row(ln_attn_b[1]))
    h = _mlp_ln(h, mlp_w_in[1].astype(BF16), mlp_w_out[1].astype(BF16),
                row(ln_mlp_g[1]), row(ln_mlp_b[1]))
    return h.reshape(batch, seq, d_model)
```

```python
import functools
import math

import jax
import jax.numpy as jnp
from jax import lax
from jax.experimental import pallas as pl
from jax.experimental.pallas import tpu as pltpu

A_HEAD_DIM = 64
A_KV_HEADS = 4
A_WINDOW = 128
B_HEAD_DIM = 128
B_KV_HEADS = 4
MOBA_BLOCK = 256
MOBA_TOPK = 3
ROPE_THETA = 10000.0
LN_EPS = 1e-5
DEPTH = 2
DEEPNORM_ALPHA = (2 * DEPTH) ** 0.25
LOG2_E = math.log2(math.e)

V7X_VMEM_BYTES = 64 * 1024 * 1024
LANES = 128
SUBLANES = 8
BF16_SUBLANES = 16

BF16 = jnp.bfloat16
F32 = jnp.float32
MASKED = -1e30

_NT = (((1,), (1,)), ((), ()))


def _resident(shape):
    return pl.BlockSpec(shape, lambda *_: (0,) * len(shape), pipeline_mode=pl.Buffered(1))


def _params(semantics, vmem_bytes):
    return pltpu.CompilerParams(dimension_semantics=semantics,
                                vmem_limit_bytes=min(int(vmem_bytes), V7X_VMEM_BYTES))


def _layer_norm(z, g, b):
    mu = jnp.mean(z, axis=-1, keepdims=True)
    zc = z - mu
    var = jnp.mean(zc * zc, axis=-1, keepdims=True)
    return zc * lax.rsqrt(var + LN_EPS) * g + b


def _rope_table_kernel(pos_ref, inv_ref, cos_ref, sin_ref):
    ang = pos_ref[...].astype(F32) * inv_ref[...]
    cos_ref[...] = jnp.cos(ang)
    sin_ref[...] = jnp.sin(ang)


def _rope_tables(pos_row, head_dim, ts=2048):
    seq = pos_row.shape[1]
    half = head_dim // 2
    inv_freq = ROPE_THETA ** (-(jnp.arange(half, dtype=F32) * 2.0) / head_dim)
    out = jax.ShapeDtypeStruct((half, seq), F32)
    return pl.pallas_call(
        _rope_table_kernel,
        grid=(seq // ts,),
        in_specs=[pl.BlockSpec((1, ts), lambda i: (0, i)),
                  pl.BlockSpec((half, 1), lambda i: (0, 0))],
        out_specs=[pl.BlockSpec((half, ts), lambda i: (0, i))] * 2,
        out_shape=[out, out],
        name="rope_tables",
    )(pos_row, inv_freq[:, None])


def _rope_rows(y, base, half, c, s):
    x1 = y[base:base + half]
    x2 = y[base + half:base + 2 * half]
    return x1 * c - x2 * s, x2 * c + x1 * s


def _proj_kernel(x_ref, w_ref, b_ref, cos_ref, sin_ref, qT_ref, kn_ref, vT_ref, *rest,
                 n_q_heads, n_kv_heads, head_dim, q_scale, kbar_block):
    half = head_dim // 2
    nq = n_q_heads * head_dim
    nk = n_kv_heads * head_dim
    xb = x_ref[...].astype(BF16)
    y = lax.dot_general(w_ref[...], xb, _NT, preferred_element_type=F32)
    y = y + b_ref[...]
    c = cos_ref[...]
    s = sin_ref[...]
    for h in range(n_q_heads):
        r1, r2 = _rope_rows(y, h * head_dim, half, c, s)
        qT_ref[h * head_dim:h * head_dim + half, :] = (r1 * q_scale).astype(BF16)
        qT_ref[h * head_dim + half:(h + 1) * head_dim, :] = (r2 * q_scale).astype(BF16)
    k_rows = []
    for g in range(n_kv_heads):
        k_rows.extend(_rope_rows(y, nq + g * head_dim, half, c, s))
    k_nat = jnp.concatenate(k_rows, axis=0).T
    kn_ref[...] = k_nat.astype(BF16)
    vT_ref[...] = y[nq + nk:nq + 2 * nk].astype(BF16)
    if kbar_block:
        (kbar_ref,) = rest
        tm = k_nat.shape[0]
        for blk in range(tm // kbar_block):
            kbar_ref[0, blk:blk + 1, :] = jnp.mean(
                k_nat[blk * kbar_block:(blk + 1) * kbar_block], axis=0, keepdims=True)


def _project(x, wT, b_col, cos, sin, *, n_q_heads, n_kv_heads, head_dim, q_scale,
             kbar_block=0, tm=512):
    seq, d_model = x.shape
    n_out = wT.shape[0]
    nq = n_q_heads * head_dim
    nk = n_kv_heads * head_dim
    half = head_dim // 2
    out_shape = [jax.ShapeDtypeStruct((nq, seq), BF16),
                 jax.ShapeDtypeStruct((seq, nk), BF16),
                 jax.ShapeDtypeStruct((nk, seq), BF16)]
    out_specs = [pl.BlockSpec((nq, tm), lambda i: (0, i)),
                 pl.BlockSpec((tm, nk), lambda i: (i, 0)),
                 pl.BlockSpec((nk, tm), lambda i: (0, i))]
    if kbar_block:
        nb = tm // kbar_block
        out_shape.append(jax.ShapeDtypeStruct((seq // tm, nb, nk), F32))
        out_specs.append(pl.BlockSpec((1, nb, nk), lambda i: (i, 0, 0)))
    vmem = (2 * tm * d_model * 4 + n_out * d_model * 2 + n_out * LANES * 4
            + 2 * n_out * tm * 2 + 4 * n_out * tm * 4)
    return pl.pallas_call(
        functools.partial(_proj_kernel, n_q_heads=n_q_heads, n_kv_heads=n_kv_heads,
                          head_dim=head_dim, q_scale=q_scale, kbar_block=kbar_block),
        grid=(seq // tm,),
        in_specs=[pl.BlockSpec((tm, d_model), lambda i: (i, 0)),
                  _resident((n_out, d_model)),
                  _resident((n_out, 1)),
                  pl.BlockSpec((half, tm), lambda i: (0, i)),
                  pl.BlockSpec((half, tm), lambda i: (0, i))],
        out_specs=out_specs,
        out_shape=out_shape,
        compiler_params=_params(("parallel",), vmem),
        name="qkv_proj_hd%d" % head_dim,
    )(x, wT, b_col, cos, sin)


def _swa_kernel(qT_ref, kprev_ref, kcur_ref, vprev_ref, vcur_ref, sink_ref, o_ref,
                sa_ref, sb_ref, mxa_ref, mxb_ref, *, n_kv_heads, group, head_dim, blk, n_sub):
    i = pl.program_id(0)
    width = group * blk
    kj = lax.broadcasted_iota(jnp.int32, (blk, width), 0)
    qi = lax.broadcasted_iota(jnp.int32, (blk, width), 1) & (blk - 1)
    in_cur = kj <= qi
    cur01 = jnp.where(in_cur, 1.0, 0.0).astype(BF16)
    no_prev_bias = jnp.where(i > 0, 0.0, MASKED)
    kv_dim = n_kv_heads * head_dim

    def score(c, g, s_ref, mx_ref):
        lo, hi = c * blk, (c + 1) * blk
        if c == 0:
            k_band = jnp.concatenate([kprev_ref[...], kcur_ref[lo:hi, :]], axis=0)
        else:
            k_band = kcur_ref[lo - blk:hi, :]
        heads = range(g * group, (g + 1) * group)
        q_g = jnp.concatenate(
            [qT_ref[h * head_dim:(h + 1) * head_dim, lo:hi] for h in heads], axis=1)
        pads = [jnp.zeros((g * head_dim, width), BF16)] if g else []
        pads.append(q_g)
        if g + 1 < n_kv_heads:
            pads.append(jnp.zeros((kv_dim - (g + 1) * head_dim, width), BF16))
        q_pad = jnp.concatenate(pads, axis=0) if len(pads) > 1 else q_g
        s_band = jnp.dot(k_band, q_pad, preferred_element_type=F32)
        s_prev = s_band[:blk] + no_prev_bias if c == 0 else s_band[:blk]
        s = jnp.where(in_cur, s_band[blk:], s_prev)
        s_ref[...] = s
        mx_ref[...] = jnp.max(s, axis=0, keepdims=True)

    def attend(c, g, s_ref, mx_ref):
        lo, hi = c * blk, (c + 1) * blk
        sink = sink_ref[:, g * width:(g + 1) * width] * LOG2_E
        m = jnp.maximum(mx_ref[...], sink)
        p = jnp.exp2(s_ref[...] - m)
        denom = jnp.sum(p, axis=0, keepdims=True) + jnp.exp2(sink - m)
        p = p.astype(BF16)
        p_cur = p * cur01
        p_band = jnp.concatenate([p - p_cur, p_cur], axis=0)
        rows = slice(g * head_dim, (g + 1) * head_dim)
        if c == 0:
            v_band = jnp.concatenate([vprev_ref[rows, :], vcur_ref[rows, lo:hi]], axis=1)
        else:
            v_band = vcur_ref[rows, lo - blk:hi]
        oT = jnp.dot(v_band, p_band, preferred_element_type=F32)
        oT = oT / denom
        o_hd = jnp.concatenate([oT[:, h * blk:(h + 1) * blk] for h in range(group)], axis=0)
        o_ref[lo:hi, g * group * head_dim:(g + 1) * group * head_dim] = o_hd.T.astype(BF16)

    todo = [(c, g) for c in range(n_sub) for g in range(n_kv_heads)]
    bufs = [(sa_ref, mxa_ref), (sb_ref, mxb_ref)]
    score(*todo[0], *bufs[0])
    for n, (c, g) in enumerate(todo):
        if n + 1 < len(todo):
            score(*todo[n + 1], *bufs[(n + 1) % 2])
        attend(c, g, *bufs[n % 2])


def _swa_attention(qT, kn, vT, sink_row, *, n_kv_heads, head_dim, blk, tq=256):
    nq, seq = qT.shape
    kv_dim = n_kv_heads * head_dim
    group = nq // kv_dim
    n_sub = tq // blk
    prev_blk = lambda i: jnp.maximum(i * n_sub - 1, 0)
    vmem = 2 * (nq * tq * 2 * 2 + 2 * (tq + blk) * kv_dim * 2) + 16 * blk * group * blk * 4
    return pl.pallas_call(
        functools.partial(_swa_kernel, n_kv_heads=n_kv_heads, group=group,
                          head_dim=head_dim, blk=blk, n_sub=n_sub),
        grid=(seq // tq,),
        in_specs=[pl.BlockSpec((nq, tq), lambda i: (0, i)),
                  pl.BlockSpec((blk, kv_dim), lambda i: (prev_blk(i), 0)),
                  pl.BlockSpec((tq, kv_dim), lambda i: (i, 0)),
                  pl.BlockSpec((kv_dim, blk), lambda i: (0, prev_blk(i))),
                  pl.BlockSpec((kv_dim, tq), lambda i: (0, i)),
                  _resident(sink_row.shape)],
        out_specs=pl.BlockSpec((tq, nq), lambda i: (i, 0)),
        out_shape=jax.ShapeDtypeStruct((seq, nq), BF16),
        scratch_shapes=[pltpu.VMEM((blk, group * blk), F32),
                        pltpu.VMEM((blk, group * blk), F32),
                        pltpu.VMEM((1, group * blk), F32),
                        pltpu.VMEM((1, group * blk), F32)],
        compiler_params=_params(("parallel",), vmem),
        name="swa_attention",
    )(qT, kn, kn, vT, vT, sink_row)


def _moba_kernel(qT_ref, kn_ref, vT_ref, kbar_ref, o_ref,
                 q_all_ref, bias_ref, sa_ref, sb_ref, mxa_ref, mxb_ref, m_ref, acc_ref,
                 *, group, head_dim, blk, topk):
    qt = pl.program_id(1)
    width = group * blk
    nb = kbar_ref.shape[0]
    own_row = bias_ref.shape[0] - nb - 1
    ones_rows = jnp.ones((acc_ref.shape[0] - head_dim, blk), BF16)
    q_all_ref[...] = jnp.concatenate(
        [qT_ref[h * head_dim:(h + 1) * head_dim, :] for h in range(group)], axis=1)
    q_all = q_all_ref[...]

    gate = jnp.dot(kbar_ref[...].astype(BF16), q_all, preferred_element_type=F32)
    row = lax.broadcasted_iota(jnp.int32, (nb, width), 0).astype(F32)
    past = row < qt.astype(F32)
    gate = jnp.where(past, gate, -jnp.inf)
    chosen = jnp.zeros((nb, width), F32)
    for _ in range(topk):
        best = jnp.max(gate, axis=0, keepdims=True)
        idx = jnp.min(jnp.where(gate == best, row, float(nb)), axis=0, keepdims=True)
        hit = row == idx
        chosen = jnp.where(hit, 1.0, chosen)
        gate = jnp.where(hit, -jnp.inf, gate)
    bias_ref[own_row:own_row + 1, :] = jnp.zeros((1, width), F32)
    bias_ref[own_row + 1:, :] = jnp.where(jnp.logical_and(chosen > 0.0, past), 0.0, MASKED)

    def scores(block):
        start = pl.multiple_of(block * blk, blk)
        return jnp.dot(kn_ref[pl.ds(start, blk), :], q_all_ref[...], preferred_element_type=F32)

    def stash(s, s_ref, mx_ref):
        s_ref[...] = s
        mx_ref[...] = jnp.max(s, axis=0, keepdims=True)

    def absorb(t, block, s_ref, mx_ref):
        bias = bias_ref[pl.ds(own_row + t, 1), :]
        m_old = m_ref[...]
        m_new = jnp.maximum(m_old, mx_ref[...] + bias)
        alpha = jnp.exp2(m_old - m_new)
        p = jnp.exp2(s_ref[...] - (m_new - bias))
        start = pl.multiple_of(block * blk, blk)
        v_ones = jnp.concatenate([vT_ref[:, pl.ds(start, blk)], ones_rows], axis=0)
        acc_ref[...] = alpha * acc_ref[...] + jnp.dot(
            v_ones, p.astype(BF16), preferred_element_type=F32)
        m_ref[...] = m_new

    m_ref[...] = jnp.full((1, width), MASKED, F32)
    acc_ref[...] = jnp.zeros(acc_ref.shape, F32)
    kj = lax.broadcasted_iota(jnp.int32, (blk, width), 0)
    qi = lax.broadcasted_iota(jnp.int32, (blk, width), 1) & (blk - 1)
    stash(jnp.where(kj <= qi, scores(qt), MASKED), sa_ref, mxa_ref)

    def absorb_positions(t, count):
        for k in range(0, count, 2):
            stash(scores(t + k), sb_ref, mxb_ref)
            absorb(t + k, jnp.where(t + k == 0, qt, t + k - 1), sa_ref, mxa_ref)
            stash(scores(jnp.minimum(t + k + 1, nb - 1)), sa_ref, mxa_ref)
            absorb(t + k + 1, t + k, sb_ref, mxb_ref)

    def quad(i, carry):
        absorb_positions(4 * i, 4)
        return carry

    def pair(i, carry):
        absorb_positions(first + 2 * i, 2)
        return carry

    n_quads = lax.shift_right_logical(qt + 1, 2)
    first = 4 * n_quads
    lax.fori_loop(0, n_quads, quad, 0)
    lax.fori_loop(0, lax.shift_right_logical(qt + 2 - first, 1), pair, 0)

    oT = acc_ref[:head_dim, :] / acc_ref[head_dim:head_dim + 1, :]
    o_hd = jnp.concatenate([oT[:, h * blk:(h + 1) * blk] for h in range(group)], axis=0)
    o_ref[...] = o_hd.T.astype(BF16)


def _moba_attention(qT, kn, vT, kbar, *, n_kv_heads, head_dim, blk, topk):
    nq, seq = qT.shape
    group = nq // (n_kv_heads * head_dim)
    width = group * blk
    nb = seq // blk
    vmem = (4 * seq * head_dim * 2 + 4 * group * head_dim * blk * 2
            + head_dim * width * (2 + 4) + 8 * blk * width * 4)
    return pl.pallas_call(
        functools.partial(_moba_kernel, group=group, head_dim=head_dim, blk=blk, topk=topk),
        grid=(n_kv_heads, nb),
        in_specs=[pl.BlockSpec((group * head_dim, blk), lambda g, t: (g, t)),
                  pl.BlockSpec((seq, head_dim), lambda g, t: (0, g)),
                  pl.BlockSpec((head_dim, seq), lambda g, t: (g, 0)),
                  pl.BlockSpec((nb, head_dim), lambda g, t: (0, g))],
        out_specs=pl.BlockSpec((blk, group * head_dim), lambda g, t: (t, g)),
        out_shape=jax.ShapeDtypeStruct((seq, nq), BF16),
        scratch_shapes=[pltpu.VMEM((head_dim, width), BF16),
                        pltpu.VMEM((nb + SUBLANES, width), F32),
                        pltpu.VMEM((blk, width), F32),
                        pltpu.VMEM((blk, width), F32),
                        pltpu.VMEM((1, width), F32),
                        pltpu.VMEM((1, width), F32),
                        pltpu.VMEM((1, width), F32),
                        pltpu.VMEM((head_dim + BF16_SUBLANES, width), F32)],
        compiler_params=_params(("parallel", "parallel"), vmem),
        name="moba_attention",
    )(qT, kn, vT, kbar)


def _oproj_ln_kernel(o_ref, w_ref, x_ref, g_ref, b_ref, h_ref):
    y = jnp.dot(o_ref[...], w_ref[...], preferred_element_type=F32)
    h_ref[...] = _layer_norm(DEEPNORM_ALPHA * x_ref[...] + y, g_ref[...], b_ref[...])


def _oproj_ln(o, w, x, g_row, b_row, tm=512):
    seq, d_model = x.shape
    k_dim = o.shape[1]
    vmem = (2 * tm * k_dim * 2 + k_dim * d_model * 2 + 4 * tm * d_model * 4
            + 3 * tm * d_model * 4)
    return pl.pallas_call(
        _oproj_ln_kernel,
        grid=(seq // tm,),
        in_specs=[pl.BlockSpec((tm, k_dim), lambda i: (i, 0)),
                  _resident((k_dim, d_model)),
                  pl.BlockSpec((tm, d_model), lambda i: (i, 0)),
                  _resident((1, d_model)),
                  _resident((1, d_model))],
        out_specs=pl.BlockSpec((tm, d_model), lambda i: (i, 0)),
        out_shape=jax.ShapeDtypeStruct((seq, d_model), F32),
        compiler_params=_params(("parallel",), vmem),
        name="oproj_ln",
    )(o, w, x, g_row, b_row)


def _mlp_ln_kernel(h_ref, win_ref, wout_ref, g_ref, b_ref, *rest, cast_next):
    if cast_next:
        nin_ref, nout_ref, out_ref, nin_bf_ref, nout_bf_ref, hb_ref, acc_ref = rest
        nin_bf_ref[...] = nin_ref[...].astype(BF16)
        nout_bf_ref[...] = nout_ref[...].astype(BF16)
    else:
        out_ref, hb_ref, acc_ref = rest
    f = pl.program_id(1)

    @pl.when(f == 0)
    def _():
        hb_ref[...] = h_ref[...].astype(BF16)
        acc_ref[...] = jnp.zeros_like(acc_ref)

    a = jnp.dot(hb_ref[...], win_ref[...], preferred_element_type=F32)
    a = jnp.maximum(a, 0.0)
    a = a * a
    acc_ref[...] += jnp.dot(a.astype(BF16), wout_ref[...], preferred_element_type=F32)

    @pl.when(f == pl.num_programs(1) - 1)
    def _():
        out_ref[...] = _layer_norm(DEEPNORM_ALPHA * h_ref[...] + acc_ref[...],
                                   g_ref[...], b_ref[...])


def _mlp_ln(h, w_in, w_out, g_row, b_row, next_f32=None, tm=512, tf=1024):
    seq, d_model = h.shape
    d_ff = w_in.shape[1]
    n_i, n_f = seq // tm, d_ff // tf
    vmem = (4 * tm * d_model * 4 + 4 * d_model * tf * 2 + tm * d_model * (2 + 4)
            + 2 * tm * tf * 4 + 2 * tm * d_model * 4)
    in_specs = [pl.BlockSpec((tm, d_model), lambda i, f: (i, 0)),
                pl.BlockSpec((d_model, tf), lambda i, f: (0, f)),
                pl.BlockSpec((tf, d_model), lambda i, f: (f, 0)),
                _resident((1, d_model)),
                _resident((1, d_model))]
    out_specs = [pl.BlockSpec((tm, d_model), lambda i, f: (i, 0))]
    out_shape = [jax.ShapeDtypeStruct((seq, d_model), F32)]
    args = [h, w_in, w_out, g_row, b_row]
    if next_f32 is not None:
        w_in_all, w_out_all, layer = next_f32
        steps = n_i * n_f
        for w_all in (w_in_all, w_out_all):
            _, rows, cols = w_all.shape
            slab = rows // steps
            in_specs.append(pl.BlockSpec((None, slab, cols), lambda i, f: (layer, i * n_f + f, 0)))
            out_specs.append(pl.BlockSpec((slab, cols), lambda i, f: (i * n_f + f, 0)))
            out_shape.append(jax.ShapeDtypeStruct((rows, cols), BF16))
            args.append(w_all)
            vmem += 2 * slab * cols * (4 + 2)
    return pl.pallas_call(
        functools.partial(_mlp_ln_kernel, cast_next=next_f32 is not None),
        grid=(n_i, n_f),
        in_specs=in_specs,
        out_specs=out_specs,
        out_shape=out_shape,
        scratch_shapes=[pltpu.VMEM((tm, d_model), BF16), pltpu.VMEM((tm, d_model), F32)],
        compiler_params=_params(("parallel", "arbitrary"), vmem),
        name="mlp_ln",
    )(*args)


def _round_kernel(w_ref, o_ref):
    o_ref[...] = w_ref[...].astype(o_ref.dtype)


def _round_layer_bf16(w_all, layer, block_bytes=8 * 1024 * 1024):
    _, rows, cols = w_all.shape
    slab = max(BF16_SUBLANES, block_bytes // (cols * 4))
    return pl.pallas_call(
        _round_kernel,
        grid=(rows // slab,),
        in_specs=[pl.BlockSpec((None, slab, cols), lambda i: (layer, i, 0))],
        out_specs=pl.BlockSpec((slab, cols), lambda i: (i, 0)),
        out_shape=jax.ShapeDtypeStruct((rows, cols), BF16),
        compiler_params=_params(("parallel",), 2 * slab * cols * (4 + 2) + 4 * slab * cols),
        name="round_bf16",
    )(w_all)


def kernel(x, positions, a_w_qkv, a_b_qkv, a_sinks, a_w_o, b_w_q, b_w_o, b_w_kv,
           ln_attn_g, ln_attn_b, ln_mlp_g, ln_mlp_b, mlp_w_in, mlp_w_out):
    batch, seq, d_model = x.shape
    assert batch == 1 and a_w_qkv.shape[0] == 1 and b_w_q.shape[0] == 1
    h = x.reshape(seq, d_model)
    pos_row = positions.reshape(1, seq)
    row = lambda v: v.reshape(1, -1)

    cos_a, sin_a = _rope_tables(pos_row, A_HEAD_DIM)
    n_a_heads = a_sinks.shape[1]
    qT, kn, vT = _project(
        h, a_w_qkv[0].T.astype(BF16), a_b_qkv[0][:, None], cos_a, sin_a,
        n_q_heads=n_a_heads, n_kv_heads=A_KV_HEADS, head_dim=A_HEAD_DIM,
        q_scale=A_HEAD_DIM ** -0.5 * LOG2_E)
    sink_row = jnp.repeat(a_sinks[0].astype(F32), A_WINDOW)[None, :]
    o = _swa_attention(qT, kn, vT, sink_row, n_kv_heads=A_KV_HEADS, head_dim=A_HEAD_DIM,
                       blk=A_WINDOW)
    h = _oproj_ln(o, a_w_o[0].astype(BF16), h, row(ln_attn_g[0]), row(ln_attn_b[0]))
    h, w_in1, w_out1 = _mlp_ln(
        h, _round_layer_bf16(mlp_w_in, 0), _round_layer_bf16(mlp_w_out, 0),
        row(ln_mlp_g[0]), row(ln_mlp_b[0]), next_f32=(mlp_w_in, mlp_w_out, 1))

    cos_b, sin_b = _rope_tables(pos_row, B_HEAD_DIM)
    w1T = jnp.concatenate([b_w_q[0].T, b_w_kv.T], axis=0).astype(BF16)
    n_b_heads = b_w_q.shape[2] // B_HEAD_DIM
    qT, kn, vT, kbar = _project(
        h, w1T, jnp.zeros((w1T.shape[0], 1), F32), cos_b, sin_b,
        n_q_heads=n_b_heads, n_kv_heads=B_KV_HEADS, head_dim=B_HEAD_DIM,
        q_scale=B_HEAD_DIM ** -0.5 * LOG2_E, kbar_block=MOBA_BLOCK)
    kbar = kbar.reshape(seq // MOBA_BLOCK, B_KV_HEADS * B_HEAD_DIM)
    o = _moba_attention(qT, kn, vT, kbar, n_kv_heads=B_KV_HEADS, head_dim=B_HEAD_DIM,
                        blk=MOBA_BLOCK, topk=MOBA_TOPK)
    h = _oproj_ln(o, b_w_o[0].astype(BF16), h, row(ln_attn_g[1]), row(ln_attn_b[1]))
    (h,) = _mlp_ln(h, w_in1, w_out1, row(ln_mlp_g[1]), row(ln_mlp_b[1]))
    return h.reshape(batch, seq, d_model)
```

```python
import functools
import math

import jax
import jax.numpy as jnp
from jax import lax
from jax.experimental import pallas as pl
from jax.experimental.pallas import tpu as pltpu

A_HEAD_DIM = 64
A_KV_HEADS = 4
A_WINDOW = 128
B_HEAD_DIM = 128
B_KV_HEADS = 4
MOBA_BLOCK = 256
MOBA_TOPK = 3
ROPE_THETA = 10000.0
LN_EPS = 1e-5
DEPTH = 2
DEEPNORM_ALPHA = (2 * DEPTH) ** 0.25
LOG2_E = math.log2(math.e)

V7X_VMEM_BYTES = 64 * 1024 * 1024
LANES = 128
SUBLANES = 8
BF16_SUBLANES = 16

BF16 = jnp.bfloat16
F32 = jnp.float32
MASKED = -1e30

_NT = (((1,), (1,)), ((), ()))


def _resident(shape):
    return pl.BlockSpec(shape, lambda *_: (0,) * len(shape), pipeline_mode=pl.Buffered(1))


def _params(semantics, vmem_bytes):
    return pltpu.CompilerParams(dimension_semantics=semantics,
                                vmem_limit_bytes=min(int(vmem_bytes), V7X_VMEM_BYTES))


class _RideAlong:
    def __init__(self, weights, layer, grid, step_of):
        steps = 1
        for n in grid:
            steps *= n
        self.args = list(weights)
        self.in_specs, self.out_specs, self.out_shape, self.vmem = [], [], [], 0
        for w in weights:
            _, rows, cols = w.shape
            slab = rows // steps
            assert slab * steps == rows and slab % BF16_SUBLANES == 0
            self.in_specs.append(
                pl.BlockSpec((None, slab, cols), lambda *g: (layer, step_of(*g), 0)))
            self.out_specs.append(pl.BlockSpec((slab, cols), lambda *g: (step_of(*g), 0)))
            self.out_shape.append(jax.ShapeDtypeStruct((rows, cols), BF16))
            self.vmem += 2 * slab * cols * (4 + 2)

    @staticmethod
    def split(refs, n_in, n_out, n_ride):
        ins, rest = refs[:n_in], refs[n_in:]
        ride_in, rest = rest[:n_ride], rest[n_ride:]
        outs, rest = rest[:n_out], rest[n_out:]
        ride_out, scratch = rest[:n_ride], rest[n_ride:]
        for src, dst in zip(ride_in, ride_out):
            dst[...] = src[...].astype(BF16)
        return ins, outs, scratch


def _layer_norm(z, g, b):
    mu = jnp.mean(z, axis=-1, keepdims=True)
    zc = z - mu
    var = jnp.mean(zc * zc, axis=-1, keepdims=True)
    return zc * lax.rsqrt(var + LN_EPS) * g + b


def _rope_table_kernel(pos_ref, inv_ref, cos_ref, sin_ref):
    ang = pos_ref[...].astype(F32) * inv_ref[...]
    cos_ref[...] = jnp.cos(ang)
    sin_ref[...] = jnp.sin(ang)


def _rope_tables(pos_row, head_dim, ts=2048):
    seq = pos_row.shape[1]
    half = head_dim // 2
    inv_freq = ROPE_THETA ** (-(jnp.arange(half, dtype=F32) * 2.0) / head_dim)
    out = jax.ShapeDtypeStruct((half, seq), F32)
    return pl.pallas_call(
        _rope_table_kernel,
        grid=(seq // ts,),
        in_specs=[pl.BlockSpec((1, ts), lambda i: (0, i)),
                  pl.BlockSpec((half, 1), lambda i: (0, 0))],
        out_specs=[pl.BlockSpec((half, ts), lambda i: (0, i))] * 2,
        out_shape=[out, out],
        name="rope_tables",
    )(pos_row, inv_freq[:, None])


def _rope_rows(y, base, half, c, s):
    x1 = y[base:base + half]
    x2 = y[base + half:base + 2 * half]
    return x1 * c - x2 * s, x2 * c + x1 * s


def _proj_kernel(*refs, w_rows, has_bias, n_q_heads, n_kv_heads, head_dim, q_scale,
                 kbar_block, n_ride, chunk=512):
    n_in = 3 + has_bias + len(w_rows)
    ins, outs, _ = _RideAlong.split(refs, n_in, 4 if kbar_block else 3, n_ride)
    x_ref, cos_ref, sin_ref = ins[:3]
    b_ref = ins[3] if has_bias else None
    w_refs = ins[3 + has_bias:]
    qT_ref, kn_ref, vT_ref = outs[:3]
    half = head_dim // 2
    nq = n_q_heads * head_dim
    nk = n_kv_heads * head_dim
    xb = x_ref[...].astype(BF16)
    c = cos_ref[...]
    s = sin_ref[...]

    def project_rows(lo, hi):
        base = 0
        for w_ref, rows in zip(w_refs, w_rows):
            if lo < base + rows:
                assert hi <= base + rows
                y = lax.dot_general(w_ref[lo - base:hi - base, :], xb, _NT,
                                    preferred_element_type=F32)
                return y + b_ref[lo:hi, :] if has_bias else y
            base += rows

    for base in range(0, nq, chunk):
        y = project_rows(base, base + chunk)
        for h in range(chunk // head_dim):
            r1, r2 = _rope_rows(y, h * head_dim, half, c, s)
            lo = base + h * head_dim
            qT_ref[lo:lo + half, :] = (r1 * q_scale).astype(BF16)
            qT_ref[lo + half:lo + head_dim, :] = (r2 * q_scale).astype(BF16)
    y = project_rows(nq, nq + nk)
    k_rows = []
    for g in range(n_kv_heads):
        k_rows.extend(_rope_rows(y, g * head_dim, half, c, s))
    k_nat = jnp.concatenate(k_rows, axis=0).T
    kn_ref[...] = k_nat.astype(BF16)
    vT_ref[...] = project_rows(nq + nk, nq + 2 * nk).astype(BF16)
    if kbar_block:
        kbar_ref = outs[3]
        tm = k_nat.shape[0]
        for blk in range(tm // kbar_block):
            kbar_ref[0, blk:blk + 1, :] = jnp.mean(
                k_nat[blk * kbar_block:(blk + 1) * kbar_block], axis=0, keepdims=True)


def _project(x, wTs, b_col, cos, sin, *, n_q_heads, n_kv_heads, head_dim, q_scale,
             kbar_block=0, ride=(), tm=512):
    seq, d_model = x.shape
    w_rows = tuple(w.shape[0] for w in wTs)
    n_out = sum(w_rows)
    bias = [] if b_col is None else [b_col]
    nq = n_q_heads * head_dim
    nk = n_kv_heads * head_dim
    half = head_dim // 2
    out_shape = [jax.ShapeDtypeStruct((nq, seq), BF16),
                 jax.ShapeDtypeStruct((seq, nk), BF16),
                 jax.ShapeDtypeStruct((nk, seq), BF16)]
    out_specs = [pl.BlockSpec((nq, tm), lambda i: (0, i)),
                 pl.BlockSpec((tm, nk), lambda i: (i, 0)),
                 pl.BlockSpec((nk, tm), lambda i: (0, i))]
    if kbar_block:
        nb = tm // kbar_block
        out_shape.append(jax.ShapeDtypeStruct((seq // tm, nb, nk), F32))
        out_specs.append(pl.BlockSpec((1, nb, nk), lambda i: (i, 0, 0)))
    grid = (seq // tm,)
    rider = _RideAlong(*ride, grid, lambda i: i) if ride else _RideAlong([], 0, grid, None)
    vmem = (2 * tm * d_model * 4 + n_out * d_model * 2 + n_out * LANES * 4
            + 2 * n_out * tm * 2 + 4 * n_out * tm * 4 + rider.vmem)
    return pl.pallas_call(
        functools.partial(_proj_kernel, w_rows=w_rows, has_bias=b_col is not None,
                          n_q_heads=n_q_heads, n_kv_heads=n_kv_heads, head_dim=head_dim,
                          q_scale=q_scale, kbar_block=kbar_block, n_ride=len(rider.args)),
        grid=grid,
        in_specs=[pl.BlockSpec((tm, d_model), lambda i: (i, 0)),
                  pl.BlockSpec((half, tm), lambda i: (0, i)),
                  pl.BlockSpec((half, tm), lambda i: (0, i))]
                 + [_resident(b.shape) for b in bias]
                 + [_resident(w.shape) for w in wTs] + rider.in_specs,
        out_specs=out_specs + rider.out_specs,
        out_shape=out_shape + rider.out_shape,
        compiler_params=_params(("parallel",), vmem),
        name="qkv_proj_hd%d" % head_dim,
    )(x, cos, sin, *bias, *wTs, *rider.args)


def _swa_kernel(*refs, n_kv_heads, group, head_dim, blk, n_sub, n_ride):
    ins, (o_ref,), scratch = _RideAlong.split(refs, 6, 1, n_ride)
    qT_ref, kprev_ref, kcur_ref, vprev_ref, vcur_ref, sink_ref = ins
    sa_ref, sb_ref, mxa_ref, mxb_ref = scratch
    i = pl.program_id(0)
    width = group * blk
    kj = lax.broadcasted_iota(jnp.int32, (blk, width), 0)
    qi = lax.broadcasted_iota(jnp.int32, (blk, width), 1) & (blk - 1)
    in_cur = kj <= qi
    cur01 = jnp.where(in_cur, 1.0, 0.0).astype(BF16)
    no_prev_bias = jnp.where(i > 0, 0.0, MASKED)
    kv_dim = n_kv_heads * head_dim

    def score(c, g, s_ref, mx_ref):
        lo, hi = c * blk, (c + 1) * blk
        if c == 0:
            k_band = jnp.concatenate([kprev_ref[...], kcur_ref[lo:hi, :]], axis=0)
        else:
            k_band = kcur_ref[lo - blk:hi, :]
        heads = range(g * group, (g + 1) * group)
        q_g = jnp.concatenate(
            [qT_ref[h * head_dim:(h + 1) * head_dim, lo:hi] for h in heads], axis=1)
        pads = [jnp.zeros((g * head_dim, width), BF16)] if g else []
        pads.append(q_g)
        if g + 1 < n_kv_heads:
            pads.append(jnp.zeros((kv_dim - (g + 1) * head_dim, width), BF16))
        q_pad = jnp.concatenate(pads, axis=0) if len(pads) > 1 else q_g
        s_band = jnp.dot(k_band, q_pad, preferred_element_type=F32)
        s_prev = s_band[:blk] + no_prev_bias if c == 0 else s_band[:blk]
        s = jnp.where(in_cur, s_band[blk:], s_prev)
        s_ref[...] = s
        mx_ref[...] = jnp.max(s, axis=0, keepdims=True)

    def attend(c, g, s_ref, mx_ref):
        lo, hi = c * blk, (c + 1) * blk
        sink = sink_ref[:, g * width:(g + 1) * width] * LOG2_E
        m = jnp.maximum(mx_ref[...], sink)
        p = jnp.exp2(s_ref[...] - m)
        denom = jnp.sum(p, axis=0, keepdims=True) + jnp.exp2(sink - m)
        p = p.astype(BF16)
        p_cur = p * cur01
        p_band = jnp.concatenate([p - p_cur, p_cur], axis=0)
        rows = slice(g * head_dim, (g + 1) * head_dim)
        if c == 0:
            v_band = jnp.concatenate([vprev_ref[rows, :], vcur_ref[rows, lo:hi]], axis=1)
        else:
            v_band = vcur_ref[rows, lo - blk:hi]
        oT = jnp.dot(v_band, p_band, preferred_element_type=F32)
        oT = oT / denom
        o_hd = jnp.concatenate([oT[:, h * blk:(h + 1) * blk] for h in range(group)], axis=0)
        o_ref[lo:hi, g * group * head_dim:(g + 1) * group * head_dim] = o_hd.T.astype(BF16)

    todo = [(c, g) for c in range(n_sub) for g in range(n_kv_heads)]
    bufs = [(sa_ref, mxa_ref), (sb_ref, mxb_ref)]
    score(*todo[0], *bufs[0])
    for n, (c, g) in enumerate(todo):
        if n + 1 < len(todo):
            score(*todo[n + 1], *bufs[(n + 1) % 2])
        attend(c, g, *bufs[n % 2])


def _swa_attention(qT, kn, vT, sink_row, *, n_kv_heads, head_dim, blk, ride=(), tq=256):
    nq, seq = qT.shape
    kv_dim = n_kv_heads * head_dim
    group = nq // kv_dim
    n_sub = tq // blk
    prev_blk = lambda i: jnp.maximum(i * n_sub - 1, 0)
    grid = (seq // tq,)
    rider = _RideAlong(*ride, grid, lambda i: i) if ride else _RideAlong([], 0, grid, None)
    vmem = (2 * (nq * tq * 2 * 2 + 2 * (tq + blk) * kv_dim * 2) + 16 * blk * group * blk * 4
            + rider.vmem)
    return pl.pallas_call(
        functools.partial(_swa_kernel, n_kv_heads=n_kv_heads, group=group,
                          head_dim=head_dim, blk=blk, n_sub=n_sub, n_ride=len(rider.args)),
        grid=grid,
        in_specs=[pl.BlockSpec((nq, tq), lambda i: (0, i)),
                  pl.BlockSpec((blk, kv_dim), lambda i: (prev_blk(i), 0)),
                  pl.BlockSpec((tq, kv_dim), lambda i: (i, 0)),
                  pl.BlockSpec((kv_dim, blk), lambda i: (0, prev_blk(i))),
                  pl.BlockSpec((kv_dim, tq), lambda i: (0, i)),
                  _resident(sink_row.shape)] + rider.in_specs,
        out_specs=[pl.BlockSpec((tq, nq), lambda i: (i, 0))] + rider.out_specs,
        out_shape=[jax.ShapeDtypeStruct((seq, nq), BF16)] + rider.out_shape,
        scratch_shapes=[pltpu.VMEM((blk, group * blk), F32),
                        pltpu.VMEM((blk, group * blk), F32),
                        pltpu.VMEM((1, group * blk), F32),
                        pltpu.VMEM((1, group * blk), F32)],
        compiler_params=_params(("parallel",), vmem),
        name="swa_attention",
    )(qT, kn, kn, vT, vT, sink_row, *rider.args)


def _moba_kernel(qT_ref, kn_ref, vT_ref, kbar_ref, o_ref,
                 q_all_ref, bias_ref, sa_ref, sb_ref, mxa_ref, mxb_ref, m_ref, acc_ref,
                 *, group, head_dim, blk, topk):
    qt = pl.program_id(1)
    width = group * blk
    nb = kbar_ref.shape[0]
    own_row = bias_ref.shape[0] - nb - 1
    ones_rows = jnp.ones((acc_ref.shape[0] - head_dim, blk), BF16)
    q_all_ref[...] = jnp.concatenate(
        [qT_ref[h * head_dim:(h + 1) * head_dim, :] for h in range(group)], axis=1)
    q_all = q_all_ref[...]

    gate = jnp.dot(kbar_ref[...].astype(BF16), q_all, preferred_element_type=F32)
    row = lax.broadcasted_iota(jnp.int32, (nb, width), 0).astype(F32)
    past = row < qt.astype(F32)
    gate = jnp.where(past, gate, -jnp.inf)
    chosen = jnp.zeros((nb, width), F32)
    for _ in range(topk):
        best = jnp.max(gate, axis=0, keepdims=True)
        idx = jnp.min(jnp.where(gate == best, row, float(nb)), axis=0, keepdims=True)
        hit = row == idx
        chosen = jnp.where(hit, 1.0, chosen)
        gate = jnp.where(hit, -jnp.inf, gate)
    bias_ref[own_row:own_row + 1, :] = jnp.zeros((1, width), F32)
    bias_ref[own_row + 1:, :] = jnp.where(jnp.logical_and(chosen > 0.0, past), 0.0, MASKED)

    def scores(block):
        start = pl.multiple_of(block * blk, blk)
        return jnp.dot(kn_ref[pl.ds(start, blk), :], q_all_ref[...], preferred_element_type=F32)

    def stash(s, s_ref, mx_ref):
        s_ref[...] = s
        mx_ref[...] = jnp.max(s, axis=0, keepdims=True)

    def absorb(t, block, s_ref, mx_ref):
        bias = bias_ref[pl.ds(own_row + t, 1), :]
        m_old = m_ref[...]
        m_new = jnp.maximum(m_old, mx_ref[...] + bias)
        alpha = jnp.exp2(m_old - m_new)
        p = jnp.exp2(s_ref[...] - (m_new - bias))
        start = pl.multiple_of(block * blk, blk)
        v_ones = jnp.concatenate([vT_ref[:, pl.ds(start, blk)], ones_rows], axis=0)
        acc_ref[...] = alpha * acc_ref[...] + jnp.dot(
            v_ones, p.astype(BF16), preferred_element_type=F32)
        m_ref[...] = m_new

    m_ref[...] = jnp.full((1, width), MASKED, F32)
    acc_ref[...] = jnp.zeros(acc_ref.shape, F32)
    kj = lax.broadcasted_iota(jnp.int32, (blk, width), 0)
    qi = lax.broadcasted_iota(jnp.int32, (blk, width), 1) & (blk - 1)
    stash(jnp.where(kj <= qi, scores(qt), MASKED), sa_ref, mxa_ref)

    def absorb_positions(t, count):
        for k in range(0, count, 2):
            stash(scores(t + k), sb_ref, mxb_ref)
            absorb(t + k, jnp.where(t + k == 0, qt, t + k - 1), sa_ref, mxa_ref)
            stash(scores(jnp.minimum(t + k + 1, nb - 1)), sa_ref, mxa_ref)
            absorb(t + k + 1, t + k, sb_ref, mxb_ref)

    def quad(i, carry):
        absorb_positions(4 * i, 4)
        return carry

    def pair(i, carry):
        absorb_positions(first + 2 * i, 2)
        return carry

    n_quads = lax.shift_right_logical(qt + 1, 2)
    first = 4 * n_quads
    lax.fori_loop(0, n_quads, quad, 0)
    lax.fori_loop(0, lax.shift_right_logical(qt + 2 - first, 1), pair, 0)

    oT = acc_ref[:head_dim, :] / acc_ref[head_dim:head_dim + 1, :]
    o_hd = jnp.concatenate([oT[:, h * blk:(h + 1) * blk] for h in range(group)], axis=0)
    o_ref[...] = o_hd.T.astype(BF16)


def _moba_attention(qT, kn, vT, kbar, *, n_kv_heads, head_dim, blk, topk):
    nq, seq = qT.shape
    group = nq // (n_kv_heads * head_dim)
    width = group * blk
    nb = seq // blk
    vmem = (4 * seq * head_dim * 2 + 4 * group * head_dim * blk * 2
            + head_dim * width * (2 + 4) + 8 * blk * width * 4)
    return pl.pallas_call(
        functools.partial(_moba_kernel, group=group, head_dim=head_dim, blk=blk, topk=topk),
        grid=(n_kv_heads, nb),
        in_specs=[pl.BlockSpec((group * head_dim, blk), lambda g, t: (g, t)),
                  pl.BlockSpec((seq, head_dim), lambda g, t: (0, g)),
                  pl.BlockSpec((head_dim, seq), lambda g, t: (g, 0)),
                  pl.BlockSpec((nb, head_dim), lambda g, t: (0, g))],
        out_specs=pl.BlockSpec((blk, group * head_dim), lambda g, t: (t, g)),
        out_shape=jax.ShapeDtypeStruct((seq, nq), BF16),
        scratch_shapes=[pltpu.VMEM((head_dim, width), BF16),
                        pltpu.VMEM((nb + SUBLANES, width), F32),
                        pltpu.VMEM((blk, width), F32),
                        pltpu.VMEM((blk, width), F32),
                        pltpu.VMEM((1, width), F32),
                        pltpu.VMEM((1, width), F32),
                        pltpu.VMEM((1, width), F32),
                        pltpu.VMEM((head_dim + BF16_SUBLANES, width), F32)],
        compiler_params=_params(("parallel", "parallel"), vmem),
        name="moba_attention",
    )(qT, kn, vT, kbar)


def _oproj_ln_kernel(o_ref, w_ref, x_ref, g_ref, b_ref, h_ref, *, row_chunks=4):
    rows = o_ref.shape[0] // row_chunks
    for r in range(row_chunks):
        sl = slice(r * rows, (r + 1) * rows)
        y = jnp.dot(o_ref[sl, :], w_ref[...], preferred_element_type=F32)
        h_ref[sl, :] = _layer_norm(DEEPNORM_ALPHA * x_ref[sl, :] + y, g_ref[...], b_ref[...])


def _oproj_ln(o, w, x, g_row, b_row, tm=512):
    seq, d_model = x.shape
    k_dim = o.shape[1]
    vmem = (2 * tm * k_dim * 2 + k_dim * d_model * 2 + 4 * tm * d_model * 4
            + 3 * tm * d_model * 4)
    return pl.pallas_call(
        _oproj_ln_kernel,
        grid=(seq // tm,),
        in_specs=[pl.BlockSpec((tm, k_dim), lambda i: (i, 0)),
                  _resident((k_dim, d_model)),
                  pl.BlockSpec((tm, d_model), lambda i: (i, 0)),
                  _resident((1, d_model)),
                  _resident((1, d_model))],
        out_specs=pl.BlockSpec((tm, d_model), lambda i: (i, 0)),
        out_shape=jax.ShapeDtypeStruct((seq, d_model), F32),
        compiler_params=_params(("parallel",), vmem),
        name="oproj_ln",
    )(o, w, x, g_row, b_row)


def _mlp_ln_kernel(*refs, n_ride, row_chunks=2):
    ins, (out_ref,), (hb_ref, acc_ref) = _RideAlong.split(refs, 5, 1, n_ride)
    h_ref, win_ref, wout_ref, g_ref, b_ref = ins
    f = pl.program_id(1)
    last = pl.num_programs(1) - 1

    @pl.when(f == 0)
    def _():
        hb_ref[...] = h_ref[...].astype(BF16)
        acc_ref[...] = jnp.zeros_like(acc_ref)

    def hidden(rows):
        a = jnp.dot(hb_ref[rows, :], win_ref[...], preferred_element_type=F32)
        a = jnp.maximum(a, 0.0)
        a = a * a
        return jnp.dot(a.astype(BF16), wout_ref[...], preferred_element_type=F32)

    @pl.when(f < last)
    def _():
        acc_ref[...] += hidden(slice(None))

    @pl.when(f == last)
    def _():
        rows = hb_ref.shape[0] // row_chunks
        for r in range(row_chunks):
            sl = slice(r * rows, (r + 1) * rows)
            y = acc_ref[sl, :] + hidden(sl)
            out_ref[sl, :] = _layer_norm(DEEPNORM_ALPHA * h_ref[sl, :] + y,
                                         g_ref[...], b_ref[...])


def _mlp_ln(h, w_in, w_out, g_row, b_row, ride=(), tm=512, tf=1024):
    seq, d_model = h.shape
    d_ff = w_in.shape[1]
    grid = (seq // tm, d_ff // tf)
    step_of = lambda i, f: i * grid[1] + f
    rider = _RideAlong(*ride, grid, step_of) if ride else _RideAlong([], 0, grid, None)
    vmem = (4 * tm * d_model * 4 + 4 * d_model * tf * 2 + tm * d_model * (2 + 4)
            + 2 * tm * tf * 4 + 2 * tm * d_model * 4 + rider.vmem)
    return pl.pallas_call(
        functools.partial(_mlp_ln_kernel, n_ride=len(rider.args)),
        grid=grid,
        in_specs=[pl.BlockSpec((tm, d_model), lambda i, f: (i, 0)),
                  pl.BlockSpec((d_model, tf), lambda i, f: (0, f)),
                  pl.BlockSpec((tf, d_model), lambda i, f: (f, 0)),
                  _resident((1, d_model)),
                  _resident((1, d_model))] + rider.in_specs,
        out_specs=[pl.BlockSpec((tm, d_model), lambda i, f: (i, 0))] + rider.out_specs,
        out_shape=[jax.ShapeDtypeStruct((seq, d_model), F32)] + rider.out_shape,
        scratch_shapes=[pltpu.VMEM((tm, d_model), BF16), pltpu.VMEM((tm, d_model), F32)],
        compiler_params=_params(("parallel", "arbitrary"), vmem),
        name="mlp_ln",
    )(h, w_in, w_out, g_row, b_row, *rider.args)


def _round_transposed_kernel(w_ref, o_ref):
    o_ref[...] = w_ref[...].T.astype(BF16)


def _round_transposed(w_all, layer, tn=512):
    _, k_dim, n_dim = w_all.shape
    return pl.pallas_call(
        _round_transposed_kernel,
        grid=(n_dim // tn,),
        in_specs=[pl.BlockSpec((None, k_dim, tn), lambda j: (layer, 0, j))],
        out_specs=pl.BlockSpec((tn, k_dim), lambda j: (j, 0)),
        out_shape=jax.ShapeDtypeStruct((n_dim, k_dim), BF16),
        compiler_params=_params(("parallel",), 2 * k_dim * tn * (4 + 2) + 2 * k_dim * tn * 4),
        name="round_transposed",
    )(w_all)


def kernel(x, positions, a_w_qkv, a_b_qkv, a_sinks, a_w_o, b_w_q, b_w_o, b_w_kv,
           ln_attn_g, ln_attn_b, ln_mlp_g, ln_mlp_b, mlp_w_in, mlp_w_out):
    batch, seq, d_model = x.shape
    assert batch == 1 and a_w_qkv.shape[0] == 1 and b_w_q.shape[0] == 1
    h = x.reshape(seq, d_model)
    pos_row = positions.reshape(1, seq)
    row = lambda v: v.reshape(1, -1)


    cos_a, sin_a = _rope_tables(pos_row, A_HEAD_DIM)
    n_a_heads = a_sinks.shape[1]
    qT, kn, vT, w_in0 = _project(
        h, [_round_transposed(a_w_qkv, 0)], a_b_qkv[0][:, None], cos_a, sin_a,
        n_q_heads=n_a_heads, n_kv_heads=A_KV_HEADS, head_dim=A_HEAD_DIM,
        q_scale=A_HEAD_DIM ** -0.5 * LOG2_E, ride=([mlp_w_in], 0))
    sink_row = jnp.repeat(a_sinks[0].astype(F32), A_WINDOW)[None, :]
    o, w_out0, w_o0 = _swa_attention(
        qT, kn, vT, sink_row, n_kv_heads=A_KV_HEADS, head_dim=A_HEAD_DIM, blk=A_WINDOW,
        ride=([mlp_w_out, a_w_o], 0))
    h = _oproj_ln(o, w_o0, h, row(ln_attn_g[0]), row(ln_attn_b[0]))
    h, w_in1, w_out1 = _mlp_ln(h, w_in0, w_out0, row(ln_mlp_g[0]), row(ln_mlp_b[0]),
                               ride=([mlp_w_in, mlp_w_out], 1))

    cos_b, sin_b = _rope_tables(pos_row, B_HEAD_DIM)
    n_b_heads = b_w_q.shape[2] // B_HEAD_DIM
    qT, kn, vT, kbar, w_o1 = _project(
        h, [_round_transposed(b_w_q, 0), _round_transposed(b_w_kv[None], 0)], None,
        cos_b, sin_b, n_q_heads=n_b_heads, n_kv_heads=B_KV_HEADS, head_dim=B_HEAD_DIM,
        q_scale=B_HEAD_DIM ** -0.5 * LOG2_E, kbar_block=MOBA_BLOCK, ride=([b_w_o], 0))
    kbar = kbar.reshape(seq // MOBA_BLOCK, B_KV_HEADS * B_HEAD_DIM)
    o = _moba_attention(qT, kn, vT, kbar, n_kv_heads=B_KV_HEADS, head_dim=B_HEAD_DIM,
                        blk=MOBA_BLOCK, topk=MOBA_TOPK)
    h = _oproj_ln(o, w_o1, h, row(ln_attn_g[1]), row(ln_attn_b[1]))
    (h,) = _mlp_ln(h, w_in1, w_out1, row(ln_mlp_g[1]), row(ln_mlp_b[1]))
    return h.reshape(batch, seq, d_model)
```

```python
import functools
import math

import jax
import jax.numpy as jnp
from jax import lax
from jax.experimental import pallas as pl
from jax.experimental.pallas import tpu as pltpu

A_HEAD_DIM = 64
A_KV_HEADS = 4
A_WINDOW = 128
B_HEAD_DIM = 128
B_KV_HEADS = 4
MOBA_BLOCK = 256
MOBA_TOPK = 3
ROPE_THETA = 10000.0
LN_EPS = 1e-5
DEPTH = 2
DEEPNORM_ALPHA = (2 * DEPTH) ** 0.25
LOG2_E = math.log2(math.e)

V7X_VMEM_BYTES = 64 * 1024 * 1024
V7X_SCOPED_VMEM_BYTES = 60 * 1024 * 1024
LANES = 128
SUBLANES = 8
BF16_SUBLANES = 16

BF16 = jnp.bfloat16
F32 = jnp.float32
MASKED = -1e30

_NT = (((1,), (1,)), ((), ()))


def _resident(shape):
    return pl.BlockSpec(shape, lambda *_: (0,) * len(shape), pipeline_mode=pl.Buffered(1))


def _params(semantics, vmem_bytes, claim_all=False):
    limit = V7X_SCOPED_VMEM_BYTES if claim_all else min(int(vmem_bytes), V7X_VMEM_BYTES)
    return pltpu.CompilerParams(dimension_semantics=semantics, vmem_limit_bytes=limit)


class _RideAlong:
    def __init__(self, weights, layer, grid, step_of):
        steps = 1
        for n in grid:
            steps *= n
        self.args = list(weights)
        self.in_specs, self.out_specs, self.out_shape, self.vmem = [], [], [], 0
        for w in weights:
            _, rows, cols = w.shape
            slab = rows // steps
            assert slab * steps == rows and slab % BF16_SUBLANES == 0
            self.in_specs.append(
                pl.BlockSpec((None, slab, cols), lambda *g: (layer, step_of(*g), 0)))
            self.out_specs.append(pl.BlockSpec((slab, cols), lambda *g: (step_of(*g), 0)))
            self.out_shape.append(jax.ShapeDtypeStruct((rows, cols), BF16))
            self.vmem += 2 * slab * cols * (4 + 2)

    @staticmethod
    def split(refs, n_in, n_out, n_ride):
        ins, rest = refs[:n_in], refs[n_in:]
        ride_in, rest = rest[:n_ride], rest[n_ride:]
        outs, rest = rest[:n_out], rest[n_out:]
        ride_out, scratch = rest[:n_ride], rest[n_ride:]
        for src, dst in zip(ride_in, ride_out):
            dst[...] = src[...].astype(BF16)
        return ins, outs, scratch


def _layer_norm(z, g, b):
    mu = jnp.mean(z, axis=-1, keepdims=True)
    zc = z - mu
    var = jnp.mean(zc * zc, axis=-1, keepdims=True)
    return zc * lax.rsqrt(var + LN_EPS) * g + b


def _rope_table_kernel(pos_ref, inv_ref, cos_ref, sin_ref):
    ang = pos_ref[...].astype(F32) * inv_ref[...]
    cos_ref[...] = jnp.cos(ang)
    sin_ref[...] = jnp.sin(ang)


def _rope_tables(pos_row, head_dim, ts=2048):
    seq = pos_row.shape[1]
    half = head_dim // 2
    inv_freq = ROPE_THETA ** (-(jnp.arange(half, dtype=F32) * 2.0) / head_dim)
    out = jax.ShapeDtypeStruct((half, seq), F32)
    return pl.pallas_call(
        _rope_table_kernel,
        grid=(seq // ts,),
        in_specs=[pl.BlockSpec((1, ts), lambda i: (0, i)),
                  pl.BlockSpec((half, 1), lambda i: (0, 0))],
        out_specs=[pl.BlockSpec((half, ts), lambda i: (0, i))] * 2,
        out_shape=[out, out],
        name="rope_tables",
    )(pos_row, inv_freq[:, None])


def _rope_rows(y, base, half, c, s):
    x1 = y[base:base + half]
    x2 = y[base + half:base + 2 * half]
    return x1 * c - x2 * s, x2 * c + x1 * s


def _proj_kernel(*refs, w_rows, has_bias, n_q_heads, n_kv_heads, head_dim, q_scale,
                 kbar_block, n_ride, chunk=512):
    n_in = 3 + has_bias + len(w_rows)
    ins, outs, _ = _RideAlong.split(refs, n_in, 4 if kbar_block else 3, n_ride)
    x_ref, cos_ref, sin_ref = ins[:3]
    b_ref = ins[3] if has_bias else None
    w_refs = ins[3 + has_bias:]
    qT_ref, kn_ref, vT_ref = outs[:3]
    half = head_dim // 2
    nq = n_q_heads * head_dim
    nk = n_kv_heads * head_dim
    xb = x_ref[...].astype(BF16)
    c = cos_ref[...]
    s = sin_ref[...]

    def project_rows(lo, hi):
        base = 0
        for w_ref, rows in zip(w_refs, w_rows):
            if lo < base + rows:
                assert hi <= base + rows
                y = lax.dot_general(w_ref[lo - base:hi - base, :], xb, _NT,
                                    preferred_element_type=F32)
                return y + b_ref[lo:hi, :] if has_bias else y
            base += rows

    for base in range(0, nq, chunk):
        y = project_rows(base, base + chunk)
        for h in range(chunk // head_dim):
            r1, r2 = _rope_rows(y, h * head_dim, half, c, s)
            lo = base + h * head_dim
            qT_ref[lo:lo + half, :] = (r1 * q_scale).astype(BF16)
            qT_ref[lo + half:lo + head_dim, :] = (r2 * q_scale).astype(BF16)
    y = project_rows(nq, nq + nk)
    k_rows = []
    for g in range(n_kv_heads):
        k_rows.extend(_rope_rows(y, g * head_dim, half, c, s))
    k_nat = jnp.concatenate(k_rows, axis=0).T
    kn_ref[...] = k_nat.astype(BF16)
    vT_ref[...] = project_rows(nq + nk, nq + 2 * nk).astype(BF16)
    if kbar_block:
        kbar_ref = outs[3]
        tm = k_nat.shape[0]
        for blk in range(tm // kbar_block):
            kbar_ref[0, blk:blk + 1, :] = jnp.mean(
                k_nat[blk * kbar_block:(blk + 1) * kbar_block], axis=0, keepdims=True)


def _project(x, wTs, b_col, cos, sin, *, n_q_heads, n_kv_heads, head_dim, q_scale,
             kbar_block=0, ride=(), tm=512):
    seq, d_model = x.shape
    w_rows = tuple(w.shape[0] for w in wTs)
    n_out = sum(w_rows)
    bias = [] if b_col is None else [b_col]
    nq = n_q_heads * head_dim
    nk = n_kv_heads * head_dim
    half = head_dim // 2
    out_shape = [jax.ShapeDtypeStruct((nq, seq), BF16),
                 jax.ShapeDtypeStruct((seq, nk), BF16),
                 jax.ShapeDtypeStruct((nk, seq), BF16)]
    out_specs = [pl.BlockSpec((nq, tm), lambda i: (0, i)),
                 pl.BlockSpec((tm, nk), lambda i: (i, 0)),
                 pl.BlockSpec((nk, tm), lambda i: (0, i))]
    if kbar_block:
        nb = tm // kbar_block
        out_shape.append(jax.ShapeDtypeStruct((seq // tm, nb, nk), F32))
        out_specs.append(pl.BlockSpec((1, nb, nk), lambda i: (i, 0, 0)))
    grid = (seq // tm,)
    rider = _RideAlong(*ride, grid, lambda i: i) if ride else _RideAlong([], 0, grid, None)
    vmem = (2 * tm * d_model * 4 + n_out * d_model * 2 + n_out * LANES * 4
            + 2 * n_out * tm * 2 + 4 * n_out * tm * 4 + rider.vmem)
    return pl.pallas_call(
        functools.partial(_proj_kernel, w_rows=w_rows, has_bias=b_col is not None,
                          n_q_heads=n_q_heads, n_kv_heads=n_kv_heads, head_dim=head_dim,
                          q_scale=q_scale, kbar_block=kbar_block, n_ride=len(rider.args)),
        grid=grid,
        in_specs=[pl.BlockSpec((tm, d_model), lambda i: (i, 0)),
                  pl.BlockSpec((half, tm), lambda i: (0, i)),
                  pl.BlockSpec((half, tm), lambda i: (0, i))]
                 + [_resident(b.shape) for b in bias]
                 + [_resident(w.shape) for w in wTs] + rider.in_specs,
        out_specs=out_specs + rider.out_specs,
        out_shape=out_shape + rider.out_shape,
        compiler_params=_params(("parallel",), vmem),
        name="qkv_proj_hd%d" % head_dim,
    )(x, cos, sin, *bias, *wTs, *rider.args)


def _swa_kernel(*refs, n_kv_heads, group, head_dim, blk, n_sub, n_ride):
    ins, (o_ref,), scratch = _RideAlong.split(refs, 6, 1, n_ride)
    qT_ref, kprev_ref, kcur_ref, vprev_ref, vcur_ref, sink_ref = ins
    sa_ref, sb_ref, mxa_ref, mxb_ref = scratch
    i = pl.program_id(0)
    width = group * blk
    kj = lax.broadcasted_iota(jnp.int32, (blk, width), 0)
    qi = lax.broadcasted_iota(jnp.int32, (blk, width), 1) & (blk - 1)
    in_cur = kj <= qi
    cur01 = jnp.where(in_cur, 1.0, 0.0).astype(BF16)
    no_prev_bias = jnp.where(i > 0, 0.0, MASKED)
    kv_dim = n_kv_heads * head_dim

    def score(c, g, s_ref, mx_ref):
        lo, hi = c * blk, (c + 1) * blk
        if c == 0:
            k_band = jnp.concatenate([kprev_ref[...], kcur_ref[lo:hi, :]], axis=0)
        else:
            k_band = kcur_ref[lo - blk:hi, :]
        heads = range(g * group, (g + 1) * group)
        q_g = jnp.concatenate(
            [qT_ref[h * head_dim:(h + 1) * head_dim, lo:hi] for h in heads], axis=1)
        pads = [jnp.zeros((g * head_dim, width), BF16)] if g else []
        pads.append(q_g)
        if g + 1 < n_kv_heads:
            pads.append(jnp.zeros((kv_dim - (g + 1) * head_dim, width), BF16))
        q_pad = jnp.concatenate(pads, axis=0) if len(pads) > 1 else q_g
        s_band = jnp.dot(k_band, q_pad, preferred_element_type=F32)
        s_prev = s_band[:blk] + no_prev_bias if c == 0 else s_band[:blk]
        s = jnp.where(in_cur, s_band[blk:], s_prev)
        s_ref[...] = s
        mx_ref[...] = jnp.max(s, axis=0, keepdims=True)

    def attend(c, g, s_ref, mx_ref):
        lo, hi = c * blk, (c + 1) * blk
        sink = sink_ref[:, g * width:(g + 1) * width] * LOG2_E
        m = jnp.maximum(mx_ref[...], sink)
        p = jnp.exp2(s_ref[...] - m)
        denom = jnp.sum(p, axis=0, keepdims=True) + jnp.exp2(sink - m)
        p = p.astype(BF16)
        p_cur = p * cur01
        p_band = jnp.concatenate([p - p_cur, p_cur], axis=0)
        rows = slice(g * head_dim, (g + 1) * head_dim)
        if c == 0:
            v_band = jnp.concatenate([vprev_ref[rows, :], vcur_ref[rows, lo:hi]], axis=1)
        else:
            v_band = vcur_ref[rows, lo - blk:hi]
        oT = jnp.dot(v_band, p_band, preferred_element_type=F32)
        oT = oT / denom
        o_hd = jnp.concatenate([oT[:, h * blk:(h + 1) * blk] for h in range(group)], axis=0)
        o_ref[lo:hi, g * group * head_dim:(g + 1) * group * head_dim] = o_hd.T.astype(BF16)

    todo = [(c, g) for c in range(n_sub) for g in range(n_kv_heads)]
    bufs = [(sa_ref, mxa_ref), (sb_ref, mxb_ref)]
    score(*todo[0], *bufs[0])
    for n, (c, g) in enumerate(todo):
        if n + 1 < len(todo):
            score(*todo[n + 1], *bufs[(n + 1) % 2])
        attend(c, g, *bufs[n % 2])


def _swa_attention(qT, kn, vT, sink_row, *, n_kv_heads, head_dim, blk, ride=(), tq=256):
    nq, seq = qT.shape
    kv_dim = n_kv_heads * head_dim
    group = nq // kv_dim
    n_sub = tq // blk
    prev_blk = lambda i: jnp.maximum(i * n_sub - 1, 0)
    grid = (seq // tq,)
    rider = _RideAlong(*ride, grid, lambda i: i) if ride else _RideAlong([], 0, grid, None)
    vmem = (2 * (nq * tq * 2 * 2 + 2 * (tq + blk) * kv_dim * 2) + 16 * blk * group * blk * 4
            + rider.vmem)
    return pl.pallas_call(
        functools.partial(_swa_kernel, n_kv_heads=n_kv_heads, group=group,
                          head_dim=head_dim, blk=blk, n_sub=n_sub, n_ride=len(rider.args)),
        grid=grid,
        in_specs=[pl.BlockSpec((nq, tq), lambda i: (0, i)),
                  pl.BlockSpec((blk, kv_dim), lambda i: (prev_blk(i), 0)),
                  pl.BlockSpec((tq, kv_dim), lambda i: (i, 0)),
                  pl.BlockSpec((kv_dim, blk), lambda i: (0, prev_blk(i))),
                  pl.BlockSpec((kv_dim, tq), lambda i: (0, i)),
                  _resident(sink_row.shape)] + rider.in_specs,
        out_specs=[pl.BlockSpec((tq, nq), lambda i: (i, 0))] + rider.out_specs,
        out_shape=[jax.ShapeDtypeStruct((seq, nq), BF16)] + rider.out_shape,
        scratch_shapes=[pltpu.VMEM((blk, group * blk), F32),
                        pltpu.VMEM((blk, group * blk), F32),
                        pltpu.VMEM((1, group * blk), F32),
                        pltpu.VMEM((1, group * blk), F32)],
        compiler_params=_params(("parallel",), vmem, claim_all=True),
        name="swa_attention",
    )(qT, kn, kn, vT, vT, sink_row, *rider.args)


def _moba_kernel(qT_ref, kn_ref, vT_ref, kbar_ref, o_ref,
                 q_all_ref, bias_ref, sa_ref, sb_ref, mxa_ref, mxb_ref, m_ref, acc_ref,
                 *, group, head_dim, blk, topk):
    qt = pl.program_id(1)
    width = group * blk
    nb = kbar_ref.shape[0]
    own_row = bias_ref.shape[0] - nb - 1
    ones_rows = jnp.ones((acc_ref.shape[0] - head_dim, blk), BF16)
    q_all_ref[...] = jnp.concatenate(
        [qT_ref[h * head_dim:(h + 1) * head_dim, :] for h in range(group)], axis=1)
    q_all = q_all_ref[...]

    gate = jnp.dot(kbar_ref[...].astype(BF16), q_all, preferred_element_type=F32)
    row = lax.broadcasted_iota(jnp.int32, (nb, width), 0).astype(F32)
    past = row < qt.astype(F32)
    gate = jnp.where(past, gate, -jnp.inf)
    chosen = jnp.zeros((nb, width), F32)
    for _ in range(topk):
        best = jnp.max(gate, axis=0, keepdims=True)
        idx = jnp.min(jnp.where(gate == best, row, float(nb)), axis=0, keepdims=True)
        hit = row == idx
        chosen = jnp.where(hit, 1.0, chosen)
        gate = jnp.where(hit, -jnp.inf, gate)
    bias_ref[own_row:own_row + 1, :] = jnp.zeros((1, width), F32)
    bias_ref[own_row + 1:, :] = jnp.where(jnp.logical_and(chosen > 0.0, past), 0.0, MASKED)

    def scores(block):
        start = pl.multiple_of(block * blk, blk)
        return jnp.dot(kn_ref[pl.ds(start, blk), :], q_all_ref[...], preferred_element_type=F32)

    def stash(s, s_ref, mx_ref):
        s_ref[...] = s
        mx_ref[...] = jnp.max(s, axis=0, keepdims=True)

    def absorb(t, block, s_ref, mx_ref):
        bias = bias_ref[pl.ds(own_row + t, 1), :]
        m_old = m_ref[...]
        m_new = jnp.maximum(m_old, mx_ref[...] + bias)
        alpha = jnp.exp2(m_old - m_new)
        p = jnp.exp2(s_ref[...] - (m_new - bias))
        start = pl.multiple_of(block * blk, blk)
        v_ones = jnp.concatenate([vT_ref[:, pl.ds(start, blk)], ones_rows], axis=0)
        acc_ref[...] = alpha * acc_ref[...] + jnp.dot(
            v_ones, p.astype(BF16), preferred_element_type=F32)
        m_ref[...] = m_new

    m_ref[...] = jnp.full((1, width), MASKED, F32)
    acc_ref[...] = jnp.zeros(acc_ref.shape, F32)
    kj = lax.broadcasted_iota(jnp.int32, (blk, width), 0)
    qi = lax.broadcasted_iota(jnp.int32, (blk, width), 1) & (blk - 1)
    stash(jnp.where(kj <= qi, scores(qt), MASKED), sa_ref, mxa_ref)

    def absorb_positions(t, count):
        for k in range(0, count, 2):
            stash(scores(t + k), sb_ref, mxb_ref)
            absorb(t + k, jnp.where(t + k == 0, qt, t + k - 1), sa_ref, mxa_ref)
            stash(scores(jnp.minimum(t + k + 1, nb - 1)), sa_ref, mxa_ref)
            absorb(t + k + 1, t + k, sb_ref, mxb_ref)

    def quad(i, carry):
        absorb_positions(4 * i, 4)
        return carry

    def pair(i, carry):
        absorb_positions(first + 2 * i, 2)
        return carry

    n_quads = lax.shift_right_logical(qt + 1, 2)
    first = 4 * n_quads
    lax.fori_loop(0, n_quads, quad, 0)
    lax.fori_loop(0, lax.shift_right_logical(qt + 2 - first, 1), pair, 0)

    oT = acc_ref[:head_dim, :] / acc_ref[head_dim:head_dim + 1, :]
    o_hd = jnp.concatenate([oT[:, h * blk:(h + 1) * blk] for h in range(group)], axis=0)
    o_ref[...] = o_hd.T.astype(BF16)


def _moba_attention(qT, kn, vT, kbar, *, n_kv_heads, head_dim, blk, topk):
    nq, seq = qT.shape
    group = nq // (n_kv_heads * head_dim)
    width = group * blk
    nb = seq // blk
    vmem = (4 * seq * head_dim * 2 + 4 * group * head_dim * blk * 2
            + head_dim * width * (2 + 4) + 8 * blk * width * 4)
    return pl.pallas_call(
        functools.partial(_moba_kernel, group=group, head_dim=head_dim, blk=blk, topk=topk),
        grid=(n_kv_heads, nb),
        in_specs=[pl.BlockSpec((group * head_dim, blk), lambda g, t: (g, t)),
                  pl.BlockSpec((seq, head_dim), lambda g, t: (0, g)),
                  pl.BlockSpec((head_dim, seq), lambda g, t: (g, 0)),
                  pl.BlockSpec((nb, head_dim), lambda g, t: (0, g))],
        out_specs=pl.BlockSpec((blk, group * head_dim), lambda g, t: (t, g)),
        out_shape=jax.ShapeDtypeStruct((seq, nq), BF16),
        scratch_shapes=[pltpu.VMEM((head_dim, width), BF16),
                        pltpu.VMEM((nb + SUBLANES, width), F32),
                        pltpu.VMEM((blk, width), F32),
                        pltpu.VMEM((blk, width), F32),
                        pltpu.VMEM((1, width), F32),
                        pltpu.VMEM((1, width), F32),
                        pltpu.VMEM((1, width), F32),
                        pltpu.VMEM((head_dim + BF16_SUBLANES, width), F32)],
        compiler_params=_params(("parallel", "parallel"), vmem, claim_all=True),
        name="moba_attention",
    )(qT, kn, vT, kbar)


def _oproj_ln_kernel(o_ref, w_ref, x_ref, g_ref, b_ref, h_ref, *, row_chunks=4):
    rows = o_ref.shape[0] // row_chunks
    for r in range(row_chunks):
        sl = slice(r * rows, (r + 1) * rows)
        y = jnp.dot(o_ref[sl, :], w_ref[...], preferred_element_type=F32)
        h_ref[sl, :] = _layer_norm(DEEPNORM_ALPHA * x_ref[sl, :] + y, g_ref[...], b_ref[...])


def _oproj_ln(o, w, x, g_row, b_row, tm=512):
    seq, d_model = x.shape
    k_dim = o.shape[1]
    vmem = (2 * tm * k_dim * 2 + k_dim * d_model * 2 + 4 * tm * d_model * 4
            + 3 * tm * d_model * 4)
    return pl.pallas_call(
        _oproj_ln_kernel,
        grid=(seq // tm,),
        in_specs=[pl.BlockSpec((tm, k_dim), lambda i: (i, 0)),
                  _resident((k_dim, d_model)),
                  pl.BlockSpec((tm, d_model), lambda i: (i, 0)),
                  _resident((1, d_model)),
                  _resident((1, d_model))],
        out_specs=pl.BlockSpec((tm, d_model), lambda i: (i, 0)),
        out_shape=jax.ShapeDtypeStruct((seq, d_model), F32),
        compiler_params=_params(("parallel",), vmem),
        name="oproj_ln",
    )(o, w, x, g_row, b_row)


def _mlp_ln_kernel(*refs, n_ride, row_chunks=4):
    ins, (out_ref,), (hb_ref,) = _RideAlong.split(refs, 5, 1, n_ride)
    h_ref, win_ref, wout_ref, g_ref, b_ref = ins
    f = pl.program_id(1)
    last = pl.num_programs(1) - 1

    @pl.when(f == 0)
    def _():
        hb_ref[...] = h_ref[...].astype(BF16)
        out_ref[...] = jnp.zeros_like(out_ref)

    def hidden(rows):
        a = jnp.dot(hb_ref[rows, :], win_ref[...], preferred_element_type=F32)
        a = jnp.maximum(a, 0.0)
        a = a * a
        return jnp.dot(a.astype(BF16), wout_ref[...], preferred_element_type=F32)

    @pl.when(f < last)
    def _():
        out_ref[...] += hidden(slice(None))

    @pl.when(f == last)
    def _():
        rows = hb_ref.shape[0] // row_chunks
        for r in range(row_chunks):
            sl = slice(r * rows, (r + 1) * rows)
            y = out_ref[sl, :] + hidden(sl)
            out_ref[sl, :] = _layer_norm(DEEPNORM_ALPHA * h_ref[sl, :] + y,
                                         g_ref[...], b_ref[...])


def _mlp_ln(h, w_in, w_out, g_row, b_row, ride=(), tm=1024, tf=512):
    seq, d_model = h.shape
    d_ff = w_in.shape[1]
    grid = (seq // tm, d_ff // tf)
    step_of = lambda i, f: i * grid[1] + f
    rider = _RideAlong(*ride, grid, step_of) if ride else _RideAlong([], 0, grid, None)
    vmem = (4 * tm * d_model * 4 + 4 * d_model * tf * 2 + tm * d_model * 2
            + 2 * tm * tf * 4 + 4 * (tm // 4) * d_model * 4 + rider.vmem)
    return pl.pallas_call(
        functools.partial(_mlp_ln_kernel, n_ride=len(rider.args)),
        grid=grid,
        in_specs=[pl.BlockSpec((tm, d_model), lambda i, f: (i, 0)),
                  pl.BlockSpec((d_model, tf), lambda i, f: (0, f)),
                  pl.BlockSpec((tf, d_model), lambda i, f: (f, 0)),
                  _resident((1, d_model)),
                  _resident((1, d_model))] + rider.in_specs,
        out_specs=[pl.BlockSpec((tm, d_model), lambda i, f: (i, 0))] + rider.out_specs,
        out_shape=[jax.ShapeDtypeStruct((seq, d_model), F32)] + rider.out_shape,
        scratch_shapes=[pltpu.VMEM((tm, d_model), BF16)],
        compiler_params=_params(("parallel", "arbitrary"), vmem),
        name="mlp_ln",
    )(h, w_in, w_out, g_row, b_row, *rider.args)


def _round_transposed_kernel(w_ref, o_ref):
    o_ref[...] = w_ref[...].T.astype(BF16)


def _round_transposed(w_all, layer, tn=512):
    _, k_dim, n_dim = w_all.shape
    return pl.pallas_call(
        _round_transposed_kernel,
        grid=(n_dim // tn,),
        in_specs=[pl.BlockSpec((None, k_dim, tn), lambda j: (layer, 0, j))],
        out_specs=pl.BlockSpec((tn, k_dim), lambda j: (j, 0)),
        out_shape=jax.ShapeDtypeStruct((n_dim, k_dim), BF16),
        compiler_params=_params(("parallel",), 0, claim_all=True),
        name="round_transposed",
    )(w_all)


def kernel(x, positions, a_w_qkv, a_b_qkv, a_sinks, a_w_o, b_w_q, b_w_o, b_w_kv,
           ln_attn_g, ln_attn_b, ln_mlp_g, ln_mlp_b, mlp_w_in, mlp_w_out):
    batch, seq, d_model = x.shape
    assert batch == 1 and a_w_qkv.shape[0] == 1 and b_w_q.shape[0] == 1
    h = x.reshape(seq, d_model)
    pos_row = positions.reshape(1, seq)
    row = lambda v: v.reshape(1, -1)


    cos_a, sin_a = _rope_tables(pos_row, A_HEAD_DIM)
    n_a_heads = a_sinks.shape[1]
    qT, kn, vT, w_in0 = _project(
        h, [_round_transposed(a_w_qkv, 0)], a_b_qkv[0][:, None], cos_a, sin_a,
        n_q_heads=n_a_heads, n_kv_heads=A_KV_HEADS, head_dim=A_HEAD_DIM,
        q_scale=A_HEAD_DIM ** -0.5 * LOG2_E, ride=([mlp_w_in], 0))
    sink_row = jnp.repeat(a_sinks[0].astype(F32), A_WINDOW)[None, :]
    o, w_out0, w_o0 = _swa_attention(
        qT, kn, vT, sink_row, n_kv_heads=A_KV_HEADS, head_dim=A_HEAD_DIM, blk=A_WINDOW,
        ride=([mlp_w_out, a_w_o], 0))
    h = _oproj_ln(o, w_o0, h, row(ln_attn_g[0]), row(ln_attn_b[0]))
    h, w_in1, w_out1 = _mlp_ln(h, w_in0, w_out0, row(ln_mlp_g[0]), row(ln_mlp_b[0]),
                               ride=([mlp_w_in, mlp_w_out], 1))

    cos_b, sin_b = _rope_tables(pos_row, B_HEAD_DIM)
    n_b_heads = b_w_q.shape[2] // B_HEAD_DIM
    qT, kn, vT, kbar, w_o1 = _project(
        h, [_round_transposed(b_w_q, 0), _round_transposed(b_w_kv[None], 0)], None,
        cos_b, sin_b, n_q_heads=n_b_heads, n_kv_heads=B_KV_HEADS, head_dim=B_HEAD_DIM,
        q_scale=B_HEAD_DIM ** -0.5 * LOG2_E, kbar_block=MOBA_BLOCK, ride=([b_w_o], 0))
    kbar = kbar.reshape(seq // MOBA_BLOCK, B_KV_HEADS * B_HEAD_DIM)
    o = _moba_attention(qT, kn, vT, kbar, n_kv_heads=B_KV_HEADS, head_dim=B_HEAD_DIM,
                        blk=MOBA_BLOCK, topk=MOBA_TOPK)
    h = _oproj_ln(o, w_o1, h, row(ln_attn_g[1]), row(ln_attn_b[1]))
    (h,) = _mlp_ln(h, w_in1, w_out1, row(ln_mlp_g[1]), row(ln_mlp_b[1]))
    return h.reshape(batch, seq, d_model)
```

```python
import functools
import math

import jax
import jax.numpy as jnp
from jax import lax
from jax.experimental import pallas as pl
from jax.experimental.pallas import tpu as pltpu

A_HEAD_DIM = 64
A_KV_HEADS = 4
A_WINDOW = 128
B_HEAD_DIM = 128
B_KV_HEADS = 4
MOBA_BLOCK = 256
MOBA_TOPK = 3
ROPE_THETA = 10000.0
LN_EPS = 1e-5
DEPTH = 2
DEEPNORM_ALPHA = (2 * DEPTH) ** 0.25
LOG2_E = math.log2(math.e)

V7X_VMEM_BYTES = 64 * 1024 * 1024
V7X_SCOPED_VMEM_BYTES = 60 * 1024 * 1024
LANES = 128
SUBLANES = 8
BF16_SUBLANES = 16

BF16 = jnp.bfloat16
F32 = jnp.float32
MASKED = -1e30

_NT = (((1,), (1,)), ((), ()))


def _resident(shape):
    return pl.BlockSpec(shape, lambda *_: (0,) * len(shape), pipeline_mode=pl.Buffered(1))


def _params(semantics, vmem_bytes, claim_all=False):
    limit = V7X_SCOPED_VMEM_BYTES if claim_all else min(int(vmem_bytes), V7X_VMEM_BYTES)
    return pltpu.CompilerParams(dimension_semantics=semantics, vmem_limit_bytes=limit)


class _RideAlong:
    def __init__(self, weights, layer, grid, step_of):
        steps = 1
        for n in grid:
            steps *= n
        self.args = list(weights)
        self.in_specs, self.out_specs, self.out_shape, self.vmem = [], [], [], 0
        for w in weights:
            _, rows, cols = w.shape
            slab = rows // steps
            assert slab * steps == rows and slab % BF16_SUBLANES == 0
            self.in_specs.append(
                pl.BlockSpec((None, slab, cols), lambda *g: (layer, step_of(*g), 0)))
            self.out_specs.append(pl.BlockSpec((slab, cols), lambda *g: (step_of(*g), 0)))
            self.out_shape.append(jax.ShapeDtypeStruct((rows, cols), BF16))
            self.vmem += 2 * slab * cols * (4 + 2)

    @staticmethod
    def split(refs, n_in, n_out, n_ride):
        ins, rest = refs[:n_in], refs[n_in:]
        ride_in, rest = rest[:n_ride], rest[n_ride:]
        outs, rest = rest[:n_out], rest[n_out:]
        ride_out, scratch = rest[:n_ride], rest[n_ride:]
        for src, dst in zip(ride_in, ride_out):
            dst[...] = src[...].astype(BF16)
        return ins, outs, scratch


def _layer_norm(z, g, b):
    mu = jnp.mean(z, axis=-1, keepdims=True)
    zc = z - mu
    var = jnp.mean(zc * zc, axis=-1, keepdims=True)
    return zc * lax.rsqrt(var + LN_EPS) * g + b


def _rope_table_kernel(pos_ref, inv_ref, cos_ref, sin_ref):
    ang = pos_ref[...].astype(F32) * inv_ref[...]
    cos_ref[...] = jnp.cos(ang)
    sin_ref[...] = jnp.sin(ang)


def _rope_tables(pos_row, head_dim, ts=2048):
    seq = pos_row.shape[1]
    half = head_dim // 2
    inv_freq = ROPE_THETA ** (-(jnp.arange(half, dtype=F32) * 2.0) / head_dim)
    out = jax.ShapeDtypeStruct((half, seq), F32)
    return pl.pallas_call(
        _rope_table_kernel,
        grid=(seq // ts,),
        in_specs=[pl.BlockSpec((1, ts), lambda i: (0, i)),
                  pl.BlockSpec((half, 1), lambda i: (0, 0))],
        out_specs=[pl.BlockSpec((half, ts), lambda i: (0, i))] * 2,
        out_shape=[out, out],
        name="rope_tables",
    )(pos_row, inv_freq[:, None])


def _rope_rows(y, base, half, c, s):
    x1 = y[base:base + half]
    x2 = y[base + half:base + 2 * half]
    return x1 * c - x2 * s, x2 * c + x1 * s


def _proj_kernel(*refs, w_rows, has_bias, n_q_heads, n_kv_heads, head_dim, q_scale,
                 kbar_block, n_ride, chunk=512):
    n_in = 3 + has_bias + len(w_rows)
    ins, outs, _ = _RideAlong.split(refs, n_in, 4 if kbar_block else 3, n_ride)
    x_ref, cos_ref, sin_ref = ins[:3]
    b_ref = ins[3] if has_bias else None
    w_refs = ins[3 + has_bias:]
    qT_ref, kn_ref, vT_ref = outs[:3]
    half = head_dim // 2
    nq = n_q_heads * head_dim
    nk = n_kv_heads * head_dim
    xb = x_ref[...].astype(BF16)
    c = cos_ref[...]
    s = sin_ref[...]

    def project_rows(lo, hi):
        base = 0
        for w_ref, rows in zip(w_refs, w_rows):
            if lo < base + rows:
                assert hi <= base + rows
                y = lax.dot_general(w_ref[lo - base:hi - base, :], xb, _NT,
                                    preferred_element_type=F32)
                return y + b_ref[lo:hi, :] if has_bias else y
            base += rows

    for base in range(0, nq, chunk):
        y = project_rows(base, base + chunk)
        for h in range(chunk // head_dim):
            r1, r2 = _rope_rows(y, h * head_dim, half, c, s)
            lo = base + h * head_dim
            qT_ref[lo:lo + half, :] = (r1 * q_scale).astype(BF16)
            qT_ref[lo + half:lo + head_dim, :] = (r2 * q_scale).astype(BF16)
    y = project_rows(nq, nq + nk)
    k_rows = []
    for g in range(n_kv_heads):
        k_rows.extend(_rope_rows(y, g * head_dim, half, c, s))
    k_nat = jnp.concatenate(k_rows, axis=0).T
    kn_ref[...] = k_nat.astype(BF16)
    vT_ref[...] = project_rows(nq + nk, nq + 2 * nk).astype(BF16)
    if kbar_block:
        kbar_ref = outs[3]
        tm = k_nat.shape[0]
        for blk in range(tm // kbar_block):
            kbar_ref[0, blk:blk + 1, :] = jnp.mean(
                k_nat[blk * kbar_block:(blk + 1) * kbar_block], axis=0, keepdims=True)


def _project(x, wTs, b_col, cos, sin, *, n_q_heads, n_kv_heads, head_dim, q_scale,
             kbar_block=0, ride=(), tm=512):
    seq, d_model = x.shape
    w_rows = tuple(w.shape[0] for w in wTs)
    n_out = sum(w_rows)
    bias = [] if b_col is None else [b_col]
    nq = n_q_heads * head_dim
    nk = n_kv_heads * head_dim
    half = head_dim // 2
    out_shape = [jax.ShapeDtypeStruct((nq, seq), BF16),
                 jax.ShapeDtypeStruct((seq, nk), BF16),
                 jax.ShapeDtypeStruct((nk, seq), BF16)]
    out_specs = [pl.BlockSpec((nq, tm), lambda i: (0, i)),
                 pl.BlockSpec((tm, nk), lambda i: (i, 0)),
                 pl.BlockSpec((nk, tm), lambda i: (0, i))]
    if kbar_block:
        nb = tm // kbar_block
        out_shape.append(jax.ShapeDtypeStruct((seq // tm, nb, nk), F32))
        out_specs.append(pl.BlockSpec((1, nb, nk), lambda i: (i, 0, 0)))
    grid = (seq // tm,)
    rider = _RideAlong(*ride, grid, lambda i: i) if ride else _RideAlong([], 0, grid, None)
    vmem = (2 * tm * d_model * 4 + n_out * d_model * 2 + n_out * LANES * 4
            + 2 * n_out * tm * 2 + 4 * n_out * tm * 4 + rider.vmem)
    return pl.pallas_call(
        functools.partial(_proj_kernel, w_rows=w_rows, has_bias=b_col is not None,
                          n_q_heads=n_q_heads, n_kv_heads=n_kv_heads, head_dim=head_dim,
                          q_scale=q_scale, kbar_block=kbar_block, n_ride=len(rider.args)),
        grid=grid,
        in_specs=[pl.BlockSpec((tm, d_model), lambda i: (i, 0)),
                  pl.BlockSpec((half, tm), lambda i: (0, i)),
                  pl.BlockSpec((half, tm), lambda i: (0, i))]
                 + [_resident(b.shape) for b in bias]
                 + [_resident(w.shape) for w in wTs] + rider.in_specs,
        out_specs=out_specs + rider.out_specs,
        out_shape=out_shape + rider.out_shape,
        compiler_params=_params(("parallel",), vmem),
        name="qkv_proj_hd%d" % head_dim,
    )(x, cos, sin, *bias, *wTs, *rider.args)


def _swa_kernel(*refs, n_kv_heads, group, head_dim, blk, n_sub, n_ride):
    ins, (o_ref,), scratch = _RideAlong.split(refs, 6, 1, n_ride)
    qT_ref, kprev_ref, kcur_ref, vprev_ref, vcur_ref, sink_ref = ins
    sa_ref, sb_ref, mxa_ref, mxb_ref = scratch
    i = pl.program_id(0)
    width = group * blk
    kj = lax.broadcasted_iota(jnp.int32, (blk, width), 0)
    qi = lax.broadcasted_iota(jnp.int32, (blk, width), 1) & (blk - 1)
    in_cur = kj <= qi
    cur01 = jnp.where(in_cur, 1.0, 0.0).astype(BF16)
    no_prev_bias = jnp.where(i > 0, 0.0, MASKED)
    kv_dim = n_kv_heads * head_dim

    def score(c, g, s_ref, mx_ref):
        lo, hi = c * blk, (c + 1) * blk
        if c == 0:
            k_band = jnp.concatenate([kprev_ref[...], kcur_ref[lo:hi, :]], axis=0)
        else:
            k_band = kcur_ref[lo - blk:hi, :]
        heads = range(g * group, (g + 1) * group)
        q_g = jnp.concatenate(
            [qT_ref[h * head_dim:(h + 1) * head_dim, lo:hi] for h in heads], axis=1)
        pads = [jnp.zeros((g * head_dim, width), BF16)] if g else []
        pads.append(q_g)
        if g + 1 < n_kv_heads:
            pads.append(jnp.zeros((kv_dim - (g + 1) * head_dim, width), BF16))
        q_pad = jnp.concatenate(pads, axis=0) if len(pads) > 1 else q_g
        s_band = jnp.dot(k_band, q_pad, preferred_element_type=F32)
        s_prev = s_band[:blk] + no_prev_bias if c == 0 else s_band[:blk]
        s = jnp.where(in_cur, s_band[blk:], s_prev)
        s_ref[...] = s
        mx_ref[...] = jnp.max(s, axis=0, keepdims=True)

    def attend(c, g, s_ref, mx_ref):
        lo, hi = c * blk, (c + 1) * blk
        sink = sink_ref[:, g * width:(g + 1) * width] * LOG2_E
        m = jnp.maximum(mx_ref[...], sink)
        p = jnp.exp2(s_ref[...] - m)
        denom = jnp.sum(p, axis=0, keepdims=True) + jnp.exp2(sink - m)
        p = p.astype(BF16)
        p_cur = p * cur01
        p_band = jnp.concatenate([p - p_cur, p_cur], axis=0)
        rows = slice(g * head_dim, (g + 1) * head_dim)
        if c == 0:
            v_band = jnp.concatenate([vprev_ref[rows, :], vcur_ref[rows, lo:hi]], axis=1)
        else:
            v_band = vcur_ref[rows, lo - blk:hi]
        oT = jnp.dot(v_band, p_band, preferred_element_type=F32)
        oT = oT / denom
        o_hd = jnp.concatenate([oT[:, h * blk:(h + 1) * blk] for h in range(group)], axis=0)
        o_ref[lo:hi, g * group * head_dim:(g + 1) * group * head_dim] = o_hd.T.astype(BF16)

    todo = [(c, g) for c in range(n_sub) for g in range(n_kv_heads)]
    bufs = [(sa_ref, mxa_ref), (sb_ref, mxb_ref)]
    score(*todo[0], *bufs[0])
    for n, (c, g) in enumerate(todo):
        if n + 1 < len(todo):
            score(*todo[n + 1], *bufs[(n + 1) % 2])
        attend(c, g, *bufs[n % 2])


def _swa_attention(qT, kn, vT, sink_row, *, n_kv_heads, head_dim, blk, ride=(), tq=256):
    nq, seq = qT.shape
    kv_dim = n_kv_heads * head_dim
    group = nq // kv_dim
    n_sub = tq // blk
    prev_blk = lambda i: jnp.maximum(i * n_sub - 1, 0)
    grid = (seq // tq,)
    rider = _RideAlong(*ride, grid, lambda i: i) if ride else _RideAlong([], 0, grid, None)
    vmem = (2 * (nq * tq * 2 * 2 + 2 * (tq + blk) * kv_dim * 2) + 16 * blk * group * blk * 4
            + rider.vmem)
    return pl.pallas_call(
        functools.partial(_swa_kernel, n_kv_heads=n_kv_heads, group=group,
                          head_dim=head_dim, blk=blk, n_sub=n_sub, n_ride=len(rider.args)),
        grid=grid,
        in_specs=[pl.BlockSpec((nq, tq), lambda i: (0, i)),
                  pl.BlockSpec((blk, kv_dim), lambda i: (prev_blk(i), 0)),
                  pl.BlockSpec((tq, kv_dim), lambda i: (i, 0)),
                  pl.BlockSpec((kv_dim, blk), lambda i: (0, prev_blk(i))),
                  pl.BlockSpec((kv_dim, tq), lambda i: (0, i)),
                  _resident(sink_row.shape)] + rider.in_specs,
        out_specs=[pl.BlockSpec((tq, nq), lambda i: (i, 0))] + rider.out_specs,
        out_shape=[jax.ShapeDtypeStruct((seq, nq), BF16)] + rider.out_shape,
        scratch_shapes=[pltpu.VMEM((blk, group * blk), F32),
                        pltpu.VMEM((blk, group * blk), F32),
                        pltpu.VMEM((1, group * blk), F32),
                        pltpu.VMEM((1, group * blk), F32)],
        compiler_params=_params(("parallel",), vmem, claim_all=True),
        name="swa_attention",
    )(qT, kn, kn, vT, vT, sink_row, *rider.args)


def _moba_kernel(qT_ref, kn_ref, vT_ref, kbar_ref, o_ref, *scratch,
                 n_kv_heads, group, head_dim, blk, topk):
    for g in range(n_kv_heads):
        q_rows = slice(g * group * head_dim, (g + 1) * group * head_dim)
        kv_cols = slice(g * head_dim, (g + 1) * head_dim)
        _moba_tile(qT_ref.at[q_rows, :], kn_ref.at[:, kv_cols], vT_ref.at[kv_cols, :],
                   kbar_ref.at[:, kv_cols], o_ref.at[:, q_rows], *scratch,
                   group=group, head_dim=head_dim, blk=blk, topk=topk)


def _moba_tile(qT_ref, kn_ref, vT_ref, kbar_ref, o_ref,
               q_all_ref, bias_ref, sa_ref, sb_ref, mxa_ref, mxb_ref, m_ref, acc_ref,
               *, group, head_dim, blk, topk):
    qt = pl.program_id(0)
    width = group * blk
    nb = kbar_ref.shape[0]
    own_row = bias_ref.shape[0] - nb - 1
    ones_rows = jnp.ones((acc_ref.shape[0] - head_dim, blk), BF16)
    q_all_ref[...] = jnp.concatenate(
        [qT_ref[h * head_dim:(h + 1) * head_dim, :] for h in range(group)], axis=1)
    q_all = q_all_ref[...]

    gate = jnp.dot(kbar_ref[...].astype(BF16), q_all, preferred_element_type=F32)
    row = lax.broadcasted_iota(jnp.int32, (nb, width), 0).astype(F32)
    past = row < qt.astype(F32)
    gate = jnp.where(past, gate, -jnp.inf)
    chosen = jnp.zeros((nb, width), F32)
    for _ in range(topk):
        best = jnp.max(gate, axis=0, keepdims=True)
        idx = jnp.min(jnp.where(gate == best, row, float(nb)), axis=0, keepdims=True)
        hit = row == idx
        chosen = jnp.where(hit, 1.0, chosen)
        gate = jnp.where(hit, -jnp.inf, gate)
    bias_ref[own_row:own_row + 1, :] = jnp.zeros((1, width), F32)
    bias_ref[own_row + 1:, :] = jnp.where(jnp.logical_and(chosen > 0.0, past), 0.0, MASKED)

    def scores(block):
        start = pl.multiple_of(block * blk, blk)
        return jnp.dot(kn_ref[pl.ds(start, blk), :], q_all_ref[...], preferred_element_type=F32)

    def stash(s, s_ref, mx_ref):
        s_ref[...] = s
        mx_ref[...] = jnp.max(s, axis=0, keepdims=True)

    def absorb(t, block, s_ref, mx_ref):
        bias = bias_ref[pl.ds(own_row + t, 1), :]
        m_old = m_ref[...]
        m_new = jnp.maximum(m_old, mx_ref[...] + bias)
        alpha = jnp.exp2(m_old - m_new)
        p = jnp.exp2(s_ref[...] - (m_new - bias))
        start = pl.multiple_of(block * blk, blk)
        v_ones = jnp.concatenate([vT_ref[:, pl.ds(start, blk)], ones_rows], axis=0)
        acc_ref[...] = alpha * acc_ref[...] + jnp.dot(
            v_ones, p.astype(BF16), preferred_element_type=F32)
        m_ref[...] = m_new

    m_ref[...] = jnp.full((1, width), MASKED, F32)
    acc_ref[...] = jnp.zeros(acc_ref.shape, F32)
    kj = lax.broadcasted_iota(jnp.int32, (blk, width), 0)
    qi = lax.broadcasted_iota(jnp.int32, (blk, width), 1) & (blk - 1)
    stash(jnp.where(kj <= qi, scores(qt), MASKED), sa_ref, mxa_ref)

    def absorb_positions(t, count):
        for k in range(0, count, 2):
            stash(scores(t + k), sb_ref, mxb_ref)
            absorb(t + k, jnp.where(t + k == 0, qt, t + k - 1), sa_ref, mxa_ref)
            stash(scores(jnp.minimum(t + k + 1, nb - 1)), sa_ref, mxa_ref)
            absorb(t + k + 1, t + k, sb_ref, mxb_ref)

    def quad(i, carry):
        absorb_positions(4 * i, 4)
        return carry

    def pair(i, carry):
        absorb_positions(first + 2 * i, 2)
        return carry

    n_quads = lax.shift_right_logical(qt + 1, 2)
    first = 4 * n_quads
    lax.fori_loop(0, n_quads, quad, 0)
    lax.fori_loop(0, lax.shift_right_logical(qt + 2 - first, 1), pair, 0)

    oT = acc_ref[:head_dim, :] / acc_ref[head_dim:head_dim + 1, :]
    o_hd = jnp.concatenate([oT[:, h * blk:(h + 1) * blk] for h in range(group)], axis=0)
    o_ref[...] = o_hd.T.astype(BF16)


def _moba_attention(qT, kn, vT, kbar, *, n_kv_heads, head_dim, blk, topk):
    nq, seq = qT.shape
    group = nq // (n_kv_heads * head_dim)
    width = group * blk
    nb = seq // blk
    kv_dim = n_kv_heads * head_dim
    vmem = (2 * seq * kv_dim * 2 + 4 * nq * blk * 2
            + head_dim * width * (2 + 4) + 8 * blk * width * 4)
    return pl.pallas_call(
        functools.partial(_moba_kernel, n_kv_heads=n_kv_heads, group=group, head_dim=head_dim,
                          blk=blk, topk=topk),
        grid=(nb,),
        in_specs=[pl.BlockSpec((nq, blk), lambda t: (0, t)),
                  _resident((seq, kv_dim)),
                  _resident((kv_dim, seq)),
                  _resident((nb, kv_dim))],
        out_specs=pl.BlockSpec((blk, nq), lambda t: (t, 0)),
        out_shape=jax.ShapeDtypeStruct((seq, nq), BF16),
        scratch_shapes=[pltpu.VMEM((head_dim, width), BF16),
                        pltpu.VMEM((nb + SUBLANES, width), F32),
                        pltpu.VMEM((blk, width), F32),
                        pltpu.VMEM((blk, width), F32),
                        pltpu.VMEM((1, width), F32),
                        pltpu.VMEM((1, width), F32),
                        pltpu.VMEM((1, width), F32),
                        pltpu.VMEM((head_dim + BF16_SUBLANES, width), F32)],
        compiler_params=_params(("parallel",), vmem, claim_all=True),
        name="moba_attention",
    )(qT, kn, vT, kbar)


def _oproj_ln_kernel(o_ref, w_ref, x_ref, g_ref, b_ref, h_ref, *, row_chunks=4):
    rows = o_ref.shape[0] // row_chunks
    for r in range(row_chunks):
        sl = slice(r * rows, (r + 1) * rows)
        y = jnp.dot(o_ref[sl, :], w_ref[...], preferred_element_type=F32)
        h_ref[sl, :] = _layer_norm(DEEPNORM_ALPHA * x_ref[sl, :] + y, g_ref[...], b_ref[...])


def _oproj_ln(o, w, x, g_row, b_row, tm=512):
    seq, d_model = x.shape
    k_dim = o.shape[1]
    vmem = (2 * tm * k_dim * 2 + k_dim * d_model * 2 + 4 * tm * d_model * 4
            + 3 * tm * d_model * 4)
    return pl.pallas_call(
        _oproj_ln_kernel,
        grid=(seq // tm,),
        in_specs=[pl.BlockSpec((tm, k_dim), lambda i: (i, 0)),
                  _resident((k_dim, d_model)),
                  pl.BlockSpec((tm, d_model), lambda i: (i, 0)),
                  _resident((1, d_model)),
                  _resident((1, d_model))],
        out_specs=pl.BlockSpec((tm, d_model), lambda i: (i, 0)),
        out_shape=jax.ShapeDtypeStruct((seq, d_model), F32),
        compiler_params=_params(("parallel",), vmem),
        name="oproj_ln",
    )(o, w, x, g_row, b_row)


def _mlp_ln_kernel(*refs, n_ride, row_chunks=2):
    ins, (out_ref,), (hb_ref, acc_ref) = _RideAlong.split(refs, 5, 1, n_ride)
    h_ref, win_ref, wout_ref, g_ref, b_ref = ins
    f = pl.program_id(1)
    last = pl.num_programs(1) - 1

    @pl.when(f == 0)
    def _():
        hb_ref[...] = h_ref[...].astype(BF16)
        acc_ref[...] = jnp.zeros_like(acc_ref)

    def hidden(rows):
        a = jnp.dot(hb_ref[rows, :], win_ref[...], preferred_element_type=F32)
        a = jnp.maximum(a, 0.0)
        a = a * a
        return jnp.dot(a.astype(BF16), wout_ref[...], preferred_element_type=F32)

    @pl.when(f < last)
    def _():
        acc_ref[...] += hidden(slice(None))

    @pl.when(f == last)
    def _():
        rows = hb_ref.shape[0] // row_chunks
        for r in range(row_chunks):
            sl = slice(r * rows, (r + 1) * rows)
            y = acc_ref[sl, :] + hidden(sl)
            out_ref[sl, :] = _layer_norm(DEEPNORM_ALPHA * h_ref[sl, :] + y,
                                         g_ref[...], b_ref[...])


def _mlp_ln(h, w_in, w_out, g_row, b_row, ride=(), tm=512, tf=1024):
    seq, d_model = h.shape
    d_ff = w_in.shape[1]
    grid = (seq // tm, d_ff // tf)
    step_of = lambda i, f: i * grid[1] + f
    rider = _RideAlong(*ride, grid, step_of) if ride else _RideAlong([], 0, grid, None)
    vmem = (4 * tm * d_model * 4 + 4 * d_model * tf * 2 + tm * d_model * (2 + 4)
            + 2 * tm * tf * 4 + 2 * tm * d_model * 4 + rider.vmem)
    return pl.pallas_call(
        functools.partial(_mlp_ln_kernel, n_ride=len(rider.args)),
        grid=grid,
        in_specs=[pl.BlockSpec((tm, d_model), lambda i, f: (i, 0)),
                  pl.BlockSpec((d_model, tf), lambda i, f: (0, f)),
                  pl.BlockSpec((tf, d_model), lambda i, f: (f, 0)),
                  _resident((1, d_model)),
                  _resident((1, d_model))] + rider.in_specs,
        out_specs=[pl.BlockSpec((tm, d_model), lambda i, f: (i, 0))] + rider.out_specs,
        out_shape=[jax.ShapeDtypeStruct((seq, d_model), F32)] + rider.out_shape,
        scratch_shapes=[pltpu.VMEM((tm, d_model), BF16), pltpu.VMEM((tm, d_model), F32)],
        compiler_params=_params(("parallel", "arbitrary"), vmem),
        name="mlp_ln",
    )(h, w_in, w_out, g_row, b_row, *rider.args)


def _round_transposed_kernel(w_ref, o_ref):
    o_ref[...] = w_ref[...].T.astype(BF16)


def _round_transposed(w_all, layer, tn=256):
    _, k_dim, n_dim = w_all.shape
    return pl.pallas_call(
        _round_transposed_kernel,
        grid=(n_dim // tn,),
        in_specs=[pl.BlockSpec((None, k_dim, tn), lambda j: (layer, 0, j))],
        out_specs=pl.BlockSpec((tn, k_dim), lambda j: (j, 0)),
        out_shape=jax.ShapeDtypeStruct((n_dim, k_dim), BF16),
        compiler_params=_params(("parallel",), 0, claim_all=True),
        name="round_transposed",
    )(w_all)


def kernel(x, positions, a_w_qkv, a_b_qkv, a_sinks, a_w_o, b_w_q, b_w_o, b_w_kv,
           ln_attn_g, ln_attn_b, ln_mlp_g, ln_mlp_b, mlp_w_in, mlp_w_out):
    batch, seq, d_model = x.shape
    assert batch == 1 and a_w_qkv.shape[0] == 1 and b_w_q.shape[0] == 1
    h = x.reshape(seq, d_model)
    pos_row = positions.reshape(1, seq)
    row = lambda v: v.reshape(1, -1)


    cos_a, sin_a = _rope_tables(pos_row, A_HEAD_DIM)
    n_a_heads = a_sinks.shape[1]
    qT, kn, vT, w_in0 = _project(
        h, [_round_transposed(a_w_qkv, 0)], a_b_qkv[0][:, None], cos_a, sin_a,
        n_q_heads=n_a_heads, n_kv_heads=A_KV_HEADS, head_dim=A_HEAD_DIM,
        q_scale=A_HEAD_DIM ** -0.5 * LOG2_E, ride=([mlp_w_in], 0))
    sink_row = jnp.repeat(a_sinks[0].astype(F32), A_WINDOW)[None, :]
    o, w_out0, w_o0 = _swa_attention(
        qT, kn, vT, sink_row, n_kv_heads=A_KV_HEADS, head_dim=A_HEAD_DIM, blk=A_WINDOW,
        ride=([mlp_w_out, a_w_o], 0))
    h = _oproj_ln(o, w_o0, h, row(ln_attn_g[0]), row(ln_attn_b[0]))
    h, w_in1, w_out1 = _mlp_ln(h, w_in0, w_out0, row(ln_mlp_g[0]), row(ln_mlp_b[0]),
                               ride=([mlp_w_in, mlp_w_out], 1))

    cos_b, sin_b = _rope_tables(pos_row, B_HEAD_DIM)
    n_b_heads = b_w_q.shape[2] // B_HEAD_DIM
    qT, kn, vT, kbar, w_o1 = _project(
        h, [_round_transposed(b_w_q, 0), _round_transposed(b_w_kv[None], 0)], None,
        cos_b, sin_b, n_q_heads=n_b_heads, n_kv_heads=B_KV_HEADS, head_dim=B_HEAD_DIM,
        q_scale=B_HEAD_DIM ** -0.5 * LOG2_E, kbar_block=MOBA_BLOCK, ride=([b_w_o], 0))
    kbar = kbar.reshape(seq // MOBA_BLOCK, B_KV_HEADS * B_HEAD_DIM)
    o = _moba_attention(qT, kn, vT, kbar, n_kv_heads=B_KV_HEADS, head_dim=B_HEAD_DIM,
                        blk=MOBA_BLOCK, topk=MOBA_TOPK)
    h = _oproj_ln(o, w_o1, h, row(ln_attn_g[1]), row(ln_attn_b[1]))
    (h,) = _mlp_ln(h, w_in1, w_out1, row(ln_mlp_g[1]), row(ln_mlp_b[1]))
    return h.reshape(batch, seq, d_model)
```

```python
import functools
import math

import jax
import jax.numpy as jnp
from jax import lax
from jax.experimental import pallas as pl
from jax.experimental.pallas import tpu as pltpu

A_HEAD_DIM = 64
A_KV_HEADS = 4
A_WINDOW = 128
B_HEAD_DIM = 128
B_KV_HEADS = 4
MOBA_BLOCK = 256
MOBA_TOPK = 3
ROPE_THETA = 10000.0
LN_EPS = 1e-5
DEPTH = 2
DEEPNORM_ALPHA = (2 * DEPTH) ** 0.25
LOG2_E = math.log2(math.e)

V7X_VMEM_BYTES = 64 * 1024 * 1024
V7X_SCOPED_VMEM_BYTES = 60 * 1024 * 1024
SUBLANES = 8
BF16_SUBLANES = 16

BF16 = jnp.bfloat16
F32 = jnp.float32
MASKED = -1e30


def _resident(shape):
    return pl.BlockSpec(shape, lambda *_: (0,) * len(shape), pipeline_mode=pl.Buffered(1))


def _params(semantics, vmem_bytes, claim_all=False):
    limit = V7X_SCOPED_VMEM_BYTES if claim_all else min(int(vmem_bytes), V7X_VMEM_BYTES)
    return pltpu.CompilerParams(dimension_semantics=semantics, vmem_limit_bytes=limit)


class _RideAlong:
    def __init__(self, weights, grid, step_of):
        steps = 1
        for n in grid:
            steps *= n
        self.args = [w for w, _ in weights]
        self.in_specs, self.out_specs, self.out_shape, self.vmem = [], [], [], 0
        for w, layer in weights:
            _, rows, cols = w.shape
            slab = rows // steps
            assert slab * steps == rows and slab % BF16_SUBLANES == 0
            self.in_specs.append(pl.BlockSpec(
                (None, slab, cols), lambda *g, layer=layer: (layer, step_of(*g), 0)))
            self.out_specs.append(pl.BlockSpec((slab, cols), lambda *g: (step_of(*g), 0)))
            self.out_shape.append(jax.ShapeDtypeStruct((rows, cols), BF16))
            self.vmem += 2 * slab * cols * (4 + 2)

    @staticmethod
    def split(refs, n_in, n_out, n_ride):
        ins, rest = refs[:n_in], refs[n_in:]
        ride_in, rest = rest[:n_ride], rest[n_ride:]
        outs, rest = rest[:n_out], rest[n_out:]
        ride_out, scratch = rest[:n_ride], rest[n_ride:]
        for src, dst in zip(ride_in, ride_out):
            dst[...] = src[...].astype(BF16)
        return ins, outs, scratch


def _layer_norm(z, g, b):
    mu = jnp.mean(z, axis=-1, keepdims=True)
    zc = z - mu
    var = jnp.mean(zc * zc, axis=-1, keepdims=True)
    return zc * lax.rsqrt(var + LN_EPS) * g + b


def _rope_table_kernel(pos_ref, inv_ref, cos_ref, sin_ref):
    ang = pos_ref[...].astype(F32) * inv_ref[...]
    cos_ref[...] = jnp.cos(ang)
    sin_ref[...] = jnp.sin(ang)


def _rope_tables(pos_row, head_dim, ts=2048):
    seq = pos_row.shape[1]
    half = head_dim // 2
    inv_freq = ROPE_THETA ** (-(jnp.arange(half, dtype=F32) * 2.0) / head_dim)
    out = jax.ShapeDtypeStruct((half, seq), F32)
    return pl.pallas_call(
        _rope_table_kernel,
        grid=(seq // ts,),
        in_specs=[pl.BlockSpec((1, ts), lambda i: (0, i)),
                  pl.BlockSpec((half, 1), lambda i: (0, 0))],
        out_specs=[pl.BlockSpec((half, ts), lambda i: (0, i))] * 2,
        out_shape=[out, out],
        name="rope_tables",
    )(pos_row, inv_freq[:, None])


def _rope_rows(y, base, half, c, s):
    x1 = y[base:base + half]
    x2 = y[base + half:base + 2 * half]
    return x1 * c - x2 * s, x2 * c + x1 * s


def _proj_kernel(*refs, w_cols, has_bias, n_q_heads, n_kv_heads, head_dim, q_scale,
                 kbar_block, n_ride, chunk=512):
    n_in = 3 + has_bias + len(w_cols)
    ins, outs, _ = _RideAlong.split(refs, n_in, 4 if kbar_block else 3, n_ride)
    x_ref, cos_ref, sin_ref = ins[:3]
    b_ref = ins[3] if has_bias else None
    w_refs = ins[3 + has_bias:]
    qT_ref, kn_ref, vT_ref = outs[:3]
    half = head_dim // 2
    nq = n_q_heads * head_dim
    nk = n_kv_heads * head_dim
    xb = x_ref[...].astype(BF16)
    c = cos_ref[...]
    s = sin_ref[...]

    def project_rows(lo, hi):
        base = 0
        for w_ref, cols in zip(w_refs, w_cols):
            if lo < base + cols:
                assert hi <= base + cols
                y = jnp.dot(xb, w_ref[:, lo - base:hi - base], preferred_element_type=F32)
                if has_bias:
                    y = y + b_ref[:, lo:hi]
                return y.T
            base += cols

    for base in range(0, nq, chunk):
        y = project_rows(base, base + chunk)
        for h in range(chunk // head_dim):
            r1, r2 = _rope_rows(y, h * head_dim, half, c, s)
            lo = base + h * head_dim
            qT_ref[lo:lo + half, :] = (r1 * q_scale).astype(BF16)
            qT_ref[lo + half:lo + head_dim, :] = (r2 * q_scale).astype(BF16)
    y = project_rows(nq, nq + nk)
    k_rows = []
    for g in range(n_kv_heads):
        k_rows.extend(_rope_rows(y, g * head_dim, half, c, s))
    k_nat = jnp.concatenate(k_rows, axis=0).T
    kn_ref[...] = k_nat.astype(BF16)
    vT_ref[...] = project_rows(nq + nk, nq + 2 * nk).astype(BF16)
    if kbar_block:
        kbar_ref = outs[3]
        tm = k_nat.shape[0]
        for blk in range(tm // kbar_block):
            kbar_ref[0, blk:blk + 1, :] = jnp.mean(
                k_nat[blk * kbar_block:(blk + 1) * kbar_block], axis=0, keepdims=True)


def _project(x, ws, b_row, cos, sin, *, n_q_heads, n_kv_heads, head_dim, q_scale,
             kbar_block=0, ride=(), tm=512):
    seq, d_model = x.shape
    w_cols = tuple(w.shape[1] for w in ws)
    n_out = sum(w_cols)
    bias = [] if b_row is None else [b_row]
    nq = n_q_heads * head_dim
    nk = n_kv_heads * head_dim
    half = head_dim // 2
    out_shape = [jax.ShapeDtypeStruct((nq, seq), BF16),
                 jax.ShapeDtypeStruct((seq, nk), BF16),
                 jax.ShapeDtypeStruct((nk, seq), BF16)]
    out_specs = [pl.BlockSpec((nq, tm), lambda i: (0, i)),
                 pl.BlockSpec((tm, nk), lambda i: (i, 0)),
                 pl.BlockSpec((nk, tm), lambda i: (0, i))]
    if kbar_block:
        nb = tm // kbar_block
        out_shape.append(jax.ShapeDtypeStruct((seq // tm, nb, nk), F32))
        out_specs.append(pl.BlockSpec((1, nb, nk), lambda i: (i, 0, 0)))
    grid = (seq // tm,)
    rider = _RideAlong(ride, grid, lambda i: i)
    vmem = (2 * tm * d_model * 4 + n_out * d_model * 2 + n_out * SUBLANES * 4
            + 2 * n_out * tm * 2 + 4 * n_out * tm * 4 + rider.vmem)
    return pl.pallas_call(
        functools.partial(_proj_kernel, w_cols=w_cols, has_bias=b_row is not None,
                          n_q_heads=n_q_heads, n_kv_heads=n_kv_heads, head_dim=head_dim,
                          q_scale=q_scale, kbar_block=kbar_block, n_ride=len(rider.args)),
        grid=grid,
        in_specs=[pl.BlockSpec((tm, d_model), lambda i: (i, 0)),
                  pl.BlockSpec((half, tm), lambda i: (0, i)),
                  pl.BlockSpec((half, tm), lambda i: (0, i))]
                 + [_resident(b.shape) for b in bias]
                 + [_resident(w.shape) for w in ws] + rider.in_specs,
        out_specs=out_specs + rider.out_specs,
        out_shape=out_shape + rider.out_shape,
        compiler_params=_params(("parallel",), vmem),
        name="qkv_proj_hd%d" % head_dim,
    )(x, cos, sin, *bias, *ws, *rider.args)


def _swa_kernel(*refs, n_kv_heads, group, head_dim, blk, n_sub, n_ride):
    ins, (o_ref,), scratch = _RideAlong.split(refs, 6, 1, n_ride)
    qT_ref, kprev_ref, kcur_ref, vprev_ref, vcur_ref, sink_ref = ins
    sa_ref, sb_ref, mxa_ref, mxb_ref = scratch
    i = pl.program_id(0)
    width = group * blk
    kj = lax.broadcasted_iota(jnp.int32, (blk, width), 0)
    qi = lax.broadcasted_iota(jnp.int32, (blk, width), 1) & (blk - 1)
    in_cur = kj <= qi
    cur01 = jnp.where(in_cur, 1.0, 0.0).astype(BF16)
    no_prev_bias = jnp.where(i > 0, 0.0, MASKED)
    kv_dim = n_kv_heads * head_dim

    def score(c, g, s_ref, mx_ref):
        lo, hi = c * blk, (c + 1) * blk
        if c == 0:
            k_band = jnp.concatenate([kprev_ref[...], kcur_ref[lo:hi, :]], axis=0)
        else:
            k_band = kcur_ref[lo - blk:hi, :]
        heads = range(g * group, (g + 1) * group)
        q_g = jnp.concatenate(
            [qT_ref[h * head_dim:(h + 1) * head_dim, lo:hi] for h in heads], axis=1)
        pads = [jnp.zeros((g * head_dim, width), BF16)] if g else []
        pads.append(q_g)
        if g + 1 < n_kv_heads:
            pads.append(jnp.zeros((kv_dim - (g + 1) * head_dim, width), BF16))
        q_pad = jnp.concatenate(pads, axis=0) if len(pads) > 1 else q_g
        s_band = jnp.dot(k_band, q_pad, preferred_element_type=F32)
        s_prev = s_band[:blk] + no_prev_bias if c == 0 else s_band[:blk]
        s = jnp.where(in_cur, s_band[blk:], s_prev)
        s_ref[...] = s
        mx_ref[...] = jnp.max(s, axis=0, keepdims=True)

    def attend(c, g, s_ref, mx_ref):
        lo, hi = c * blk, (c + 1) * blk
        sink = sink_ref[:, g * width:(g + 1) * width] * LOG2_E
        m = jnp.maximum(mx_ref[...], sink)
        p = jnp.exp2(s_ref[...] - m)
        denom = jnp.sum(p, axis=0, keepdims=True) + jnp.exp2(sink - m)
        p = p.astype(BF16)
        p_cur = p * cur01
        p_band = jnp.concatenate([p - p_cur, p_cur], axis=0)
        rows = slice(g * head_dim, (g + 1) * head_dim)
        if c == 0:
            v_band = jnp.concatenate([vprev_ref[rows, :], vcur_ref[rows, lo:hi]], axis=1)
        else:
            v_band = vcur_ref[rows, lo - blk:hi]
        oT = jnp.dot(v_band, p_band, preferred_element_type=F32)
        oT = oT / denom
        o_hd = jnp.concatenate([oT[:, h * blk:(h + 1) * blk] for h in range(group)], axis=0)
        o_ref[lo:hi, g * group * head_dim:(g + 1) * group * head_dim] = o_hd.T.astype(BF16)

    todo = [(c, g) for c in range(n_sub) for g in range(n_kv_heads)]
    bufs = [(sa_ref, mxa_ref), (sb_ref, mxb_ref)]
    score(*todo[0], *bufs[0])
    for n, (c, g) in enumerate(todo):
        if n + 1 < len(todo):
            score(*todo[n + 1], *bufs[(n + 1) % 2])
        attend(c, g, *bufs[n % 2])


def _swa_attention(qT, kn, vT, sink_row, *, n_kv_heads, head_dim, blk, ride=(), tq=256):
    nq, seq = qT.shape
    kv_dim = n_kv_heads * head_dim
    group = nq // kv_dim
    n_sub = tq // blk
    prev_blk = lambda i: jnp.maximum(i * n_sub - 1, 0)
    grid = (seq // tq,)
    rider = _RideAlong(ride, grid, lambda i: i)
    vmem = (2 * (nq * tq * 2 * 2 + 2 * (tq + blk) * kv_dim * 2) + 16 * blk * group * blk * 4
            + rider.vmem)
    return pl.pallas_call(
        functools.partial(_swa_kernel, n_kv_heads=n_kv_heads, group=group,
                          head_dim=head_dim, blk=blk, n_sub=n_sub, n_ride=len(rider.args)),
        grid=grid,
        in_specs=[pl.BlockSpec((nq, tq), lambda i: (0, i)),
                  pl.BlockSpec((blk, kv_dim), lambda i: (prev_blk(i), 0)),
                  pl.BlockSpec((tq, kv_dim), lambda i: (i, 0)),
                  pl.BlockSpec((kv_dim, blk), lambda i: (0, prev_blk(i))),
                  pl.BlockSpec((kv_dim, tq), lambda i: (0, i)),
                  _resident(sink_row.shape)] + rider.in_specs,
        out_specs=[pl.BlockSpec((tq, nq), lambda i: (i, 0))] + rider.out_specs,
        out_shape=[jax.ShapeDtypeStruct((seq, nq), BF16)] + rider.out_shape,
        scratch_shapes=[pltpu.VMEM((blk, group * blk), F32),
                        pltpu.VMEM((blk, group * blk), F32),
                        pltpu.VMEM((1, group * blk), F32),
                        pltpu.VMEM((1, group * blk), F32)],
        compiler_params=_params(("parallel",), vmem, claim_all=True),
        name="swa_attention",
    )(qT, kn, kn, vT, vT, sink_row, *rider.args)


def _moba_kernel(qT_ref, kn_ref, vT_ref, kbar_ref, o_ref, *scratch,
                 n_kv_heads, group, head_dim, blk, topk):
    for g in range(n_kv_heads):
        q_rows = slice(g * group * head_dim, (g + 1) * group * head_dim)
        kv_cols = slice(g * head_dim, (g + 1) * head_dim)
        _moba_tile(qT_ref.at[q_rows, :], kn_ref.at[:, kv_cols], vT_ref.at[kv_cols, :],
                   kbar_ref.at[:, kv_cols], o_ref.at[:, q_rows], *scratch,
                   group=group, head_dim=head_dim, blk=blk, topk=topk)


def _moba_tile(qT_ref, kn_ref, vT_ref, kbar_ref, o_ref,
               q_all_ref, bias_ref, sa_ref, sb_ref, mxa_ref, mxb_ref, m_ref, acc_ref,
               *, group, head_dim, blk, topk):
    qt = pl.program_id(0)
    width = group * blk
    nb = kbar_ref.shape[0]
    own_row = bias_ref.shape[0] - nb - 1
    ones_rows = jnp.ones((acc_ref.shape[0] - head_dim, blk), BF16)
    q_all_ref[...] = jnp.concatenate(
        [qT_ref[h * head_dim:(h + 1) * head_dim, :] for h in range(group)], axis=1)
    q_all = q_all_ref[...]

    gate = jnp.dot(kbar_ref[...].astype(BF16), q_all, preferred_element_type=F32)
    row = lax.broadcasted_iota(jnp.int32, (nb, width), 0).astype(F32)
    past = row < qt.astype(F32)
    gate = jnp.where(past, gate, -jnp.inf)
    chosen = jnp.zeros((nb, width), F32)
    for _ in range(topk):
        best = jnp.max(gate, axis=0, keepdims=True)
        idx = jnp.min(jnp.where(gate == best, row, float(nb)), axis=0, keepdims=True)
        hit = row == idx
        chosen = jnp.where(hit, 1.0, chosen)
        gate = jnp.where(hit, -jnp.inf, gate)
    bias_ref[own_row:own_row + 1, :] = jnp.zeros((1, width), F32)
    bias_ref[own_row + 1:, :] = jnp.where(jnp.logical_and(chosen > 0.0, past), 0.0, MASKED)

    def scores(block):
        start = pl.multiple_of(block * blk, blk)
        return jnp.dot(kn_ref[pl.ds(start, blk), :], q_all_ref[...], preferred_element_type=F32)

    def stash(s, s_ref, mx_ref):
        s_ref[...] = s
        mx_ref[...] = jnp.max(s, axis=0, keepdims=True)

    def absorb(t, block, s_ref, mx_ref):
        bias = bias_ref[pl.ds(own_row + t, 1), :]
        m_old = m_ref[...]
        m_new = jnp.maximum(m_old, mx_ref[...] + bias)
        alpha = jnp.exp2(m_old - m_new)
        p = jnp.exp2(s_ref[...] - (m_new - bias))
        start = pl.multiple_of(block * blk, blk)
        v_ones = jnp.concatenate([vT_ref[:, pl.ds(start, blk)], ones_rows], axis=0)
        acc_ref[...] = alpha * acc_ref[...] + jnp.dot(
            v_ones, p.astype(BF16), preferred_element_type=F32)
        m_ref[...] = m_new

    m_ref[...] = jnp.full((1, width), MASKED, F32)
    acc_ref[...] = jnp.zeros(acc_ref.shape, F32)
    kj = lax.broadcasted_iota(jnp.int32, (blk, width), 0)
    qi = lax.broadcasted_iota(jnp.int32, (blk, width), 1) & (blk - 1)
    stash(jnp.where(kj <= qi, scores(qt), MASKED), sa_ref, mxa_ref)

    def block_of(t):
        return jnp.where(t == 0, qt, t - 1)

    def absorb_pairs(t, count):
        for k in range(0, count, 2):
            stash(scores(t + k), sb_ref, mxb_ref)
            absorb(t + k, block_of(t + k), sa_ref, mxa_ref)
            stash(scores(t + k + 1), sa_ref, mxa_ref)
            absorb(t + k + 1, t + k, sb_ref, mxb_ref)

    def quad(i, carry):
        absorb_pairs(4 * i, 4)
        return carry

    n_quads = lax.shift_right_logical(qt, 2)
    rest = qt & 3
    lax.fori_loop(0, n_quads, quad, 0)
    t_rest = 4 * n_quads

    @pl.when(rest >= 2)
    def _():
        absorb_pairs(t_rest, 2)

    @pl.when((rest & 1) == 1)
    def _():
        t = t_rest + (rest & 2)
        stash(scores(t), sb_ref, mxb_ref)
        absorb(t, block_of(t), sa_ref, mxa_ref)
        absorb(t + 1, t, sb_ref, mxb_ref)

    @pl.when((rest & 1) == 0)
    def _():
        absorb(qt, block_of(qt), sa_ref, mxa_ref)

    oT = acc_ref[:head_dim, :] / acc_ref[head_dim:head_dim + 1, :]
    o_hd = jnp.concatenate([oT[:, h * blk:(h + 1) * blk] for h in range(group)], axis=0)
    o_ref[...] = o_hd.T.astype(BF16)


def _moba_attention(qT, kn, vT, kbar, *, n_kv_heads, head_dim, blk, topk):
    nq, seq = qT.shape
    group = nq // (n_kv_heads * head_dim)
    width = group * blk
    nb = seq // blk
    kv_dim = n_kv_heads * head_dim
    vmem = (2 * seq * kv_dim * 2 + 4 * nq * blk * 2
            + head_dim * width * (2 + 4) + 8 * blk * width * 4)
    return pl.pallas_call(
        functools.partial(_moba_kernel, n_kv_heads=n_kv_heads, group=group, head_dim=head_dim,
                          blk=blk, topk=topk),
        grid=(nb,),
        in_specs=[pl.BlockSpec((nq, blk), lambda t: (0, t)),
                  _resident((seq, kv_dim)),
                  _resident((kv_dim, seq)),
                  _resident((nb, kv_dim))],
        out_specs=pl.BlockSpec((blk, nq), lambda t: (t, 0)),
        out_shape=jax.ShapeDtypeStruct((seq, nq), BF16),
        scratch_shapes=[pltpu.VMEM((head_dim, width), BF16),
                        pltpu.VMEM((nb + SUBLANES, width), F32),
                        pltpu.VMEM((blk, width), F32),
                        pltpu.VMEM((blk, width), F32),
                        pltpu.VMEM((1, width), F32),
                        pltpu.VMEM((1, width), F32),
                        pltpu.VMEM((1, width), F32),
                        pltpu.VMEM((head_dim + BF16_SUBLANES, width), F32)],
        compiler_params=_params(("parallel",), vmem, claim_all=True),
        name="moba_attention",
    )(qT, kn, vT, kbar)


def _oproj_ln_kernel(o_ref, w_ref, x_ref, g_ref, b_ref, h_ref, *, row_chunks=4):
    rows = o_ref.shape[0] // row_chunks
    for r in range(row_chunks):
        sl = slice(r * rows, (r + 1) * rows)
        y = jnp.dot(o_ref[sl, :], w_ref[...], preferred_element_type=F32)
        h_ref[sl, :] = _layer_norm(DEEPNORM_ALPHA * x_ref[sl, :] + y, g_ref[...], b_ref[...])


def _oproj_ln(o, w, x, g_row, b_row, tm=512):
    seq, d_model = x.shape
    k_dim = o.shape[1]
    vmem = (2 * tm * k_dim * 2 + k_dim * d_model * 2 + 4 * tm * d_model * 4
            + 3 * tm * d_model * 4)
    return pl.pallas_call(
        _oproj_ln_kernel,
        grid=(seq // tm,),
        in_specs=[pl.BlockSpec((tm, k_dim), lambda i: (i, 0)),
                  _resident((k_dim, d_model)),
                  pl.BlockSpec((tm, d_model), lambda i: (i, 0)),
                  _resident((1, d_model)),
                  _resident((1, d_model))],
        out_specs=pl.BlockSpec((tm, d_model), lambda i: (i, 0)),
        out_shape=jax.ShapeDtypeStruct((seq, d_model), F32),
        compiler_params=_params(("parallel",), vmem),
        name="oproj_ln",
    )(o, w, x, g_row, b_row)


def _mlp_ln_kernel(*refs, n_ride, row_chunks=2):
    ins, (out_ref,), (hb_ref, acc_ref) = _RideAlong.split(refs, 5, 1, n_ride)
    h_ref, win_ref, wout_ref, g_ref, b_ref = ins
    f = pl.program_id(1)
    last = pl.num_programs(1) - 1

    @pl.when(f == 0)
    def _():
        hb_ref[...] = h_ref[...].astype(BF16)
        acc_ref[...] = jnp.zeros_like(acc_ref)

    def hidden(rows):
        a = jnp.dot(hb_ref[rows, :], win_ref[...], preferred_element_type=F32)
        a = jnp.maximum(a, 0.0)
        a = a * a
        return jnp.dot(a.astype(BF16), wout_ref[...], preferred_element_type=F32)

    @pl.when(f < last)
    def _():
        acc_ref[...] += hidden(slice(None))

    @pl.when(f == last)
    def _():
        rows = hb_ref.shape[0] // row_chunks
        for r in range(row_chunks):
            sl = slice(r * rows, (r + 1) * rows)
            y = acc_ref[sl, :] + hidden(sl)
            out_ref[sl, :] = _layer_norm(DEEPNORM_ALPHA * h_ref[sl, :] + y,
                                         g_ref[...], b_ref[...])


def _mlp_ln(h, w_in, w_out, g_row, b_row, ride=(), tm=512, tf=1024):
    seq, d_model = h.shape
    d_ff = w_in.shape[1]
    grid = (seq // tm, d_ff // tf)
    step_of = lambda i, f: i * grid[1] + f
    rider = _RideAlong(ride, grid, step_of)
    vmem = (4 * tm * d_model * 4 + 4 * d_model * tf * 2 + tm * d_model * (2 + 4)
            + 2 * tm * tf * 4 + 2 * tm * d_model * 4 + rider.vmem)
    return pl.pallas_call(
        functools.partial(_mlp_ln_kernel, n_ride=len(rider.args)),
        grid=grid,
        in_specs=[pl.BlockSpec((tm, d_model), lambda i, f: (i, 0)),
                  pl.BlockSpec((d_model, tf), lambda i, f: (0, f)),
                  pl.BlockSpec((tf, d_model), lambda i, f: (f, 0)),
                  _resident((1, d_model)),
                  _resident((1, d_model))] + rider.in_specs,
        out_specs=[pl.BlockSpec((tm, d_model), lambda i, f: (i, 0))] + rider.out_specs,
        out_shape=[jax.ShapeDtypeStruct((seq, d_model), F32)] + rider.out_shape,
        scratch_shapes=[pltpu.VMEM((tm, d_model), BF16), pltpu.VMEM((tm, d_model), F32)],
        compiler_params=_params(("parallel", "arbitrary"), vmem),
        name="mlp_ln",
    )(h, w_in, w_out, g_row, b_row, *rider.args)


def _round_kernel(w_ref, o_ref):
    o_ref[...] = w_ref[...].astype(BF16)


def _round_layer(w_all, layer, slab=256):
    _, rows, cols = w_all.shape
    return pl.pallas_call(
        _round_kernel,
        grid=(rows // slab,),
        in_specs=[pl.BlockSpec((None, slab, cols), lambda i: (layer, i, 0))],
        out_specs=pl.BlockSpec((slab, cols), lambda i: (i, 0)),
        out_shape=jax.ShapeDtypeStruct((rows, cols), BF16),
        compiler_params=_params(("parallel",), 0, claim_all=True),
        name="round_layer",
    )(w_all)


def kernel(x, positions, a_w_qkv, a_b_qkv, a_sinks, a_w_o, b_w_q, b_w_o, b_w_kv,
           ln_attn_g, ln_attn_b, ln_mlp_g, ln_mlp_b, mlp_w_in, mlp_w_out):
    batch, seq, d_model = x.shape
    assert batch == 1 and a_w_qkv.shape[0] == 1 and b_w_q.shape[0] == 1
    h = x.reshape(seq, d_model)
    pos_row = positions.reshape(1, seq)
    row = lambda v: v.reshape(1, -1)


    cos_a, sin_a = _rope_tables(pos_row, A_HEAD_DIM)
    n_a_heads = a_sinks.shape[1]
    qT, kn, vT, w_in0 = _project(
        h, [_round_layer(a_w_qkv, 0)], a_b_qkv, cos_a, sin_a,
        n_q_heads=n_a_heads, n_kv_heads=A_KV_HEADS, head_dim=A_HEAD_DIM,
        q_scale=A_HEAD_DIM ** -0.5 * LOG2_E, ride=[(mlp_w_in, 0)])
    sink_row = jnp.repeat(a_sinks[0].astype(F32), A_WINDOW)[None, :]
    o, w_out0, w_o0 = _swa_attention(
        qT, kn, vT, sink_row, n_kv_heads=A_KV_HEADS, head_dim=A_HEAD_DIM, blk=A_WINDOW,
        ride=[(mlp_w_out, 0), (a_w_o, 0)])
    h = _oproj_ln(o, w_o0, h, row(ln_attn_g[0]), row(ln_attn_b[0]))
    h, w_in1, w_out1, w_q1, w_kv1 = _mlp_ln(
        h, w_in0, w_out0, row(ln_mlp_g[0]), row(ln_mlp_b[0]),
        ride=[(mlp_w_in, 1), (mlp_w_out, 1), (b_w_q, 0), (b_w_kv[None], 0)])

    cos_b, sin_b = _rope_tables(pos_row, B_HEAD_DIM)
    n_b_heads = b_w_q.shape[2] // B_HEAD_DIM
    qT, kn, vT, kbar, w_o1 = _project(
        h, [w_q1, w_kv1], None,
        cos_b, sin_b, n_q_heads=n_b_heads, n_kv_heads=B_KV_HEADS, head_dim=B_HEAD_DIM,
        q_scale=B_HEAD_DIM ** -0.5 * LOG2_E, kbar_block=MOBA_BLOCK, ride=[(b_w_o, 0)])
    kbar = kbar.reshape(seq // MOBA_BLOCK, B_KV_HEADS * B_HEAD_DIM)
    o = _moba_attention(qT, kn, vT, kbar, n_kv_heads=B_KV_HEADS, head_dim=B_HEAD_DIM,
                        blk=MOBA_BLOCK, topk=MOBA_TOPK)
    h = _oproj_ln(o, w_o1, h, row(ln_attn_g[1]), row(ln_attn_b[1]))
    (h,) = _mlp_ln(h, w_in1, w_out1, row(ln_mlp_g[1]), row(ln_mlp_b[1]))
    return h.reshape(batch, seq, d_model)
```

```python
import functools
import math

import jax
import jax.numpy as jnp
from jax import lax
from jax.experimental import pallas as pl
from jax.experimental.pallas import tpu as pltpu

A_HEAD_DIM = 64
A_KV_HEADS = 4
A_WINDOW = 128
B_HEAD_DIM = 128
B_KV_HEADS = 4
MOBA_BLOCK = 256
MOBA_TOPK = 3
ROPE_THETA = 10000.0
LN_EPS = 1e-5
DEPTH = 2
DEEPNORM_ALPHA = (2 * DEPTH) ** 0.25
LOG2_E = math.log2(math.e)

V7X_VMEM_BYTES = 64 * 1024 * 1024
V7X_SCOPED_VMEM_BYTES = 60 * 1024 * 1024
SUBLANES = 8
BF16_SUBLANES = 16

BF16 = jnp.bfloat16
F32 = jnp.float32
MASKED = -1e30


def _resident(shape):
    return pl.BlockSpec(shape, lambda *_: (0,) * len(shape), pipeline_mode=pl.Buffered(1))


def _params(semantics, vmem_bytes, claim_all=False):
    limit = V7X_SCOPED_VMEM_BYTES if claim_all else min(int(vmem_bytes), V7X_VMEM_BYTES)
    return pltpu.CompilerParams(dimension_semantics=semantics, vmem_limit_bytes=limit)


class _RideAlong:
    def __init__(self, weights, grid, step_of):
        steps = 1
        for n in grid:
            steps *= n
        self.args = [w for w, _ in weights]
        self.in_specs, self.out_specs, self.out_shape, self.vmem = [], [], [], 0
        for w, layer in weights:
            _, rows, cols = w.shape
            slab = rows // steps
            assert slab * steps == rows and slab % BF16_SUBLANES == 0
            self.in_specs.append(pl.BlockSpec(
                (None, slab, cols), lambda *g, layer=layer: (layer, step_of(*g), 0)))
            self.out_specs.append(pl.BlockSpec((slab, cols), lambda *g: (step_of(*g), 0)))
            self.out_shape.append(jax.ShapeDtypeStruct((rows, cols), BF16))
            self.vmem += 2 * slab * cols * (4 + 2)

    @staticmethod
    def split(refs, n_in, n_out, n_ride):
        ins, rest = refs[:n_in], refs[n_in:]
        ride_in, rest = rest[:n_ride], rest[n_ride:]
        outs, rest = rest[:n_out], rest[n_out:]
        ride_out, scratch = rest[:n_ride], rest[n_ride:]
        for src, dst in zip(ride_in, ride_out):
            dst[...] = src[...].astype(BF16)
        return ins, outs, scratch


def _layer_norm(z, g, b):
    mu = jnp.mean(z, axis=-1, keepdims=True)
    zc = z - mu
    var = jnp.mean(zc * zc, axis=-1, keepdims=True)
    return zc * lax.rsqrt(var + LN_EPS) * g + b


def _rope_table_kernel(pos_ref, inv_ref, cos_ref, sin_ref):
    ang = pos_ref[...].astype(F32) * inv_ref[...]
    cos_ref[...] = jnp.cos(ang)
    sin_ref[...] = jnp.sin(ang)


def _rope_tables(pos_row, head_dim, ts=2048):
    seq = pos_row.shape[1]
    half = head_dim // 2
    inv_freq = ROPE_THETA ** (-(jnp.arange(half, dtype=F32) * 2.0) / head_dim)
    out = jax.ShapeDtypeStruct((half, seq), F32)
    return pl.pallas_call(
        _rope_table_kernel,
        grid=(seq // ts,),
        in_specs=[pl.BlockSpec((1, ts), lambda i: (0, i)),
                  pl.BlockSpec((half, 1), lambda i: (0, 0))],
        out_specs=[pl.BlockSpec((half, ts), lambda i: (0, i))] * 2,
        out_shape=[out, out],
        name="rope_tables",
    )(pos_row, inv_freq[:, None])


def _rope_rows(y, base, half, c, s):
    x1 = y[base:base + half]
    x2 = y[base + half:base + 2 * half]
    return x1 * c - x2 * s, x2 * c + x1 * s


def _proj_kernel(*refs, w_cols, has_bias, n_q_heads, n_kv_heads, head_dim, q_scale,
                 kbar_block, n_ride, chunk=512):
    n_in = 3 + has_bias + len(w_cols)
    ins, outs, _ = _RideAlong.split(refs, n_in, 4 if kbar_block else 3, n_ride)
    x_ref, cos_ref, sin_ref = ins[:3]
    b_ref = ins[3] if has_bias else None
    w_refs = ins[3 + has_bias:]
    qT_ref, kn_ref, vT_ref = outs[:3]
    half = head_dim // 2
    nq = n_q_heads * head_dim
    nk = n_kv_heads * head_dim
    xb = x_ref[...].astype(BF16)
    c = cos_ref[...]
    s = sin_ref[...]

    def project_rows(lo, hi):
        base = 0
        for w_ref, cols in zip(w_refs, w_cols):
            if lo < base + cols:
                assert hi <= base + cols
                y = jnp.dot(xb, w_ref[:, lo - base:hi - base], preferred_element_type=F32)
                if has_bias:
                    y = y + b_ref[:, lo:hi]
                return y.T
            base += cols

    for base in range(0, nq, chunk):
        y = project_rows(base, base + chunk)
        for h in range(chunk // head_dim):
            r1, r2 = _rope_rows(y, h * head_dim, half, c, s)
            lo = base + h * head_dim
            qT_ref[lo:lo + half, :] = (r1 * q_scale).astype(BF16)
            qT_ref[lo + half:lo + head_dim, :] = (r2 * q_scale).astype(BF16)
    y = project_rows(nq, nq + nk)
    k_rows = []
    for g in range(n_kv_heads):
        k_rows.extend(_rope_rows(y, g * head_dim, half, c, s))
    k_nat = jnp.concatenate(k_rows, axis=0).T
    kn_ref[...] = k_nat.astype(BF16)
    vT_ref[...] = project_rows(nq + nk, nq + 2 * nk).astype(BF16)
    if kbar_block:
        kbar_ref = outs[3]
        tm = k_nat.shape[0]
        for blk in range(tm // kbar_block):
            kbar_ref[0, blk:blk + 1, :] = jnp.mean(
                k_nat[blk * kbar_block:(blk + 1) * kbar_block], axis=0, keepdims=True)


def _project(x, ws, b_row, cos, sin, *, n_q_heads, n_kv_heads, head_dim, q_scale,
             kbar_block=0, ride=(), tm=512):
    seq, d_model = x.shape
    w_cols = tuple(w.shape[1] for w in ws)
    n_out = sum(w_cols)
    bias = [] if b_row is None else [b_row]
    nq = n_q_heads * head_dim
    nk = n_kv_heads * head_dim
    half = head_dim // 2
    out_shape = [jax.ShapeDtypeStruct((nq, seq), BF16),
                 jax.ShapeDtypeStruct((seq, nk), BF16),
                 jax.ShapeDtypeStruct((nk, seq), BF16)]
    out_specs = [pl.BlockSpec((nq, tm), lambda i: (0, i)),
                 pl.BlockSpec((tm, nk), lambda i: (i, 0)),
                 pl.BlockSpec((nk, tm), lambda i: (0, i))]
    if kbar_block:
        nb = tm // kbar_block
        out_shape.append(jax.ShapeDtypeStruct((seq // tm, nb, nk), F32))
        out_specs.append(pl.BlockSpec((1, nb, nk), lambda i: (i, 0, 0)))
    grid = (seq // tm,)
    rider = _RideAlong(ride, grid, lambda i: i)
    vmem = (2 * tm * d_model * 4 + n_out * d_model * 2 + n_out * SUBLANES * 4
            + 2 * n_out * tm * 2 + 4 * n_out * tm * 4 + rider.vmem)
    return pl.pallas_call(
        functools.partial(_proj_kernel, w_cols=w_cols, has_bias=b_row is not None,
                          n_q_heads=n_q_heads, n_kv_heads=n_kv_heads, head_dim=head_dim,
                          q_scale=q_scale, kbar_block=kbar_block, n_ride=len(rider.args)),
        grid=grid,
        in_specs=[pl.BlockSpec((tm, d_model), lambda i: (i, 0)),
                  pl.BlockSpec((half, tm), lambda i: (0, i)),
                  pl.BlockSpec((half, tm), lambda i: (0, i))]
                 + [_resident(b.shape) for b in bias]
                 + [_resident(w.shape) for w in ws] + rider.in_specs,
        out_specs=out_specs + rider.out_specs,
        out_shape=out_shape + rider.out_shape,
        compiler_params=_params(("parallel",), vmem),
        name="qkv_proj_hd%d" % head_dim,
    )(x, cos, sin, *bias, *ws, *rider.args)


def _swa_kernel(*refs, n_kv_heads, group, head_dim, blk, n_sub, n_ride):
    ins, (o_ref,), scratch = _RideAlong.split(refs, 6, 1, n_ride)
    qT_ref, kprev_ref, kcur_ref, vprev_ref, vcur_ref, sink_ref = ins
    sa_ref, sb_ref, mxa_ref, mxb_ref = scratch
    i = pl.program_id(0)
    width = group * blk
    kj = lax.broadcasted_iota(jnp.int32, (blk, width), 0)
    qi = lax.broadcasted_iota(jnp.int32, (blk, width), 1) & (blk - 1)
    in_cur = kj <= qi
    cur01 = jnp.where(in_cur, 1.0, 0.0).astype(BF16)
    no_prev_bias = jnp.where(i > 0, 0.0, MASKED)
    kv_dim = n_kv_heads * head_dim

    def score(c, g, s_ref, mx_ref):
        lo, hi = c * blk, (c + 1) * blk
        if c == 0:
            k_band = jnp.concatenate([kprev_ref[...], kcur_ref[lo:hi, :]], axis=0)
        else:
            k_band = kcur_ref[lo - blk:hi, :]
        heads = range(g * group, (g + 1) * group)
        q_g = jnp.concatenate(
            [qT_ref[h * head_dim:(h + 1) * head_dim, lo:hi] for h in heads], axis=1)
        pads = [jnp.zeros((g * head_dim, width), BF16)] if g else []
        pads.append(q_g)
        if g + 1 < n_kv_heads:
            pads.append(jnp.zeros((kv_dim - (g + 1) * head_dim, width), BF16))
        q_pad = jnp.concatenate(pads, axis=0) if len(pads) > 1 else q_g
        s_band = jnp.dot(k_band, q_pad, preferred_element_type=F32)
        s_prev = s_band[:blk] + no_prev_bias if c == 0 else s_band[:blk]
        s = jnp.where(in_cur, s_band[blk:], s_prev)
        s_ref[...] = s
        mx_ref[...] = jnp.max(s, axis=0, keepdims=True)

    def attend(c, g, s_ref, mx_ref):
        lo, hi = c * blk, (c + 1) * blk
        sink = sink_ref[:, g * width:(g + 1) * width] * LOG2_E
        m = jnp.maximum(mx_ref[...], sink)
        p = jnp.exp2(s_ref[...] - m)
        denom = jnp.sum(p, axis=0, keepdims=True) + jnp.exp2(sink - m)
        p = p.astype(BF16)
        p_cur = p * cur01
        p_band = jnp.concatenate([p - p_cur, p_cur], axis=0)
        rows = slice(g * head_dim, (g + 1) * head_dim)
        if c == 0:
            v_band = jnp.concatenate([vprev_ref[rows, :], vcur_ref[rows, lo:hi]], axis=1)
        else:
            v_band = vcur_ref[rows, lo - blk:hi]
        oT = jnp.dot(v_band, p_band, preferred_element_type=F32)
        oT = oT / denom
        o_hd = jnp.concatenate([oT[:, h * blk:(h + 1) * blk] for h in range(group)], axis=0)
        o_ref[lo:hi, g * group * head_dim:(g + 1) * group * head_dim] = o_hd.T.astype(BF16)

    todo = [(c, g) for c in range(n_sub) for g in range(n_kv_heads)]
    bufs = [(sa_ref, mxa_ref), (sb_ref, mxb_ref)]
    score(*todo[0], *bufs[0])
    for n, (c, g) in enumerate(todo):
        if n + 1 < len(todo):
            score(*todo[n + 1], *bufs[(n + 1) % 2])
        attend(c, g, *bufs[n % 2])


def _swa_attention(qT, kn, vT, sink_row, *, n_kv_heads, head_dim, blk, ride=(), tq=256):
    nq, seq = qT.shape
    kv_dim = n_kv_heads * head_dim
    group = nq // kv_dim
    n_sub = tq // blk
    prev_blk = lambda i: jnp.maximum(i * n_sub - 1, 0)
    grid = (seq // tq,)
    rider = _RideAlong(ride, grid, lambda i: i)
    vmem = (2 * (nq * tq * 2 * 2 + 2 * (tq + blk) * kv_dim * 2) + 16 * blk * group * blk * 4
            + rider.vmem)
    return pl.pallas_call(
        functools.partial(_swa_kernel, n_kv_heads=n_kv_heads, group=group,
                          head_dim=head_dim, blk=blk, n_sub=n_sub, n_ride=len(rider.args)),
        grid=grid,
        in_specs=[pl.BlockSpec((nq, tq), lambda i: (0, i)),
                  pl.BlockSpec((blk, kv_dim), lambda i: (prev_blk(i), 0)),
                  pl.BlockSpec((tq, kv_dim), lambda i: (i, 0)),
                  pl.BlockSpec((kv_dim, blk), lambda i: (0, prev_blk(i))),
                  pl.BlockSpec((kv_dim, tq), lambda i: (0, i)),
                  _resident(sink_row.shape)] + rider.in_specs,
        out_specs=[pl.BlockSpec((tq, nq), lambda i: (i, 0))] + rider.out_specs,
        out_shape=[jax.ShapeDtypeStruct((seq, nq), BF16)] + rider.out_shape,
        scratch_shapes=[pltpu.VMEM((blk, group * blk), F32),
                        pltpu.VMEM((blk, group * blk), F32),
                        pltpu.VMEM((1, group * blk), F32),
                        pltpu.VMEM((1, group * blk), F32)],
        compiler_params=_params(("parallel",), vmem, claim_all=True),
        name="swa_attention",
    )(qT, kn, kn, vT, vT, sink_row, *rider.args)


def _moba_kernel(*refs, n_kv_heads, group, head_dim, blk, topk, n_ride):
    (qT_ref, kn_ref, vT_ref, kbar_ref), (o_ref,), scratch = _RideAlong.split(refs, 4, 1, n_ride)
    for g in range(n_kv_heads):
        q_rows = slice(g * group * head_dim, (g + 1) * group * head_dim)
        kv_cols = slice(g * head_dim, (g + 1) * head_dim)
        _moba_tile(qT_ref.at[q_rows, :], kn_ref.at[:, kv_cols], vT_ref.at[kv_cols, :],
                   kbar_ref.at[:, kv_cols], o_ref.at[:, q_rows], *scratch,
                   group=group, head_dim=head_dim, blk=blk, topk=topk)


def _moba_tile(qT_ref, kn_ref, vT_ref, kbar_ref, o_ref,
               q_all_ref, bias_ref, sa_ref, sb_ref, mxa_ref, mxb_ref, m_ref, acc_ref,
               *, group, head_dim, blk, topk):
    qt = pl.program_id(0)
    width = group * blk
    nb = kbar_ref.shape[0]
    own_row = bias_ref.shape[0] - nb - 1
    ones_rows = jnp.ones((acc_ref.shape[0] - head_dim, blk), BF16)
    q_all_ref[...] = jnp.concatenate(
        [qT_ref[h * head_dim:(h + 1) * head_dim, :] for h in range(group)], axis=1)
    q_all = q_all_ref[...]

    gate = jnp.dot(kbar_ref[...].astype(BF16), q_all, preferred_element_type=F32)
    row = lax.broadcasted_iota(jnp.int32, (nb, width), 0).astype(F32)
    past = row < qt.astype(F32)
    gate = jnp.where(past, gate, -jnp.inf)
    chosen = jnp.zeros((nb, width), F32)
    for _ in range(topk):
        best = jnp.max(gate, axis=0, keepdims=True)
        idx = jnp.min(jnp.where(gate == best, row, float(nb)), axis=0, keepdims=True)
        hit = row == idx
        chosen = jnp.where(hit, 1.0, chosen)
        gate = jnp.where(hit, -jnp.inf, gate)
    bias_ref[own_row:own_row + 1, :] = jnp.zeros((1, width), F32)
    bias_ref[own_row + 1:, :] = jnp.where(jnp.logical_and(chosen > 0.0, past), 0.0, MASKED)

    def scores(block):
        start = pl.multiple_of(block * blk, blk)
        return jnp.dot(kn_ref[pl.ds(start, blk), :], q_all_ref[...], preferred_element_type=F32)

    def stash(s, s_ref, mx_ref):
        s_ref[...] = s
        mx_ref[...] = jnp.max(s, axis=0, keepdims=True)

    def absorb(t, block, s_ref, mx_ref):
        bias = bias_ref[pl.ds(own_row + t, 1), :]
        m_old = m_ref[...]
        m_new = jnp.maximum(m_old, mx_ref[...] + bias)
        alpha = jnp.exp2(m_old - m_new)
        p = jnp.exp2(s_ref[...] - (m_new - bias))
        start = pl.multiple_of(block * blk, blk)
        v_ones = jnp.concatenate([vT_ref[:, pl.ds(start, blk)], ones_rows], axis=0)
        acc_ref[...] = alpha * acc_ref[...] + jnp.dot(
            v_ones, p.astype(BF16), preferred_element_type=F32)
        m_ref[...] = m_new

    m_ref[...] = jnp.full((1, width), MASKED, F32)
    acc_ref[...] = jnp.zeros(acc_ref.shape, F32)
    kj = lax.broadcasted_iota(jnp.int32, (blk, width), 0)
    qi = lax.broadcasted_iota(jnp.int32, (blk, width), 1) & (blk - 1)
    stash(jnp.where(kj <= qi, scores(qt), MASKED), sa_ref, mxa_ref)

    def block_of(t):
        return jnp.where(t == 0, qt, t - 1)

    def absorb_pairs(t, count):
        for k in range(0, count, 2):
            stash(scores(t + k), sb_ref, mxb_ref)
            absorb(t + k, block_of(t + k), sa_ref, mxa_ref)
            stash(scores(t + k + 1), sa_ref, mxa_ref)
            absorb(t + k + 1, t + k, sb_ref, mxb_ref)

    def quad(i, carry):
        absorb_pairs(4 * i, 4)
        return carry

    n_quads = lax.shift_right_logical(qt, 2)
    rest = qt & 3
    lax.fori_loop(0, n_quads, quad, 0)
    t_rest = 4 * n_quads

    @pl.when(rest >= 2)
    def _():
        absorb_pairs(t_rest, 2)

    @pl.when((rest & 1) == 1)
    def _():
        t = t_rest + (rest & 2)
        stash(scores(t), sb_ref, mxb_ref)
        absorb(t, block_of(t), sa_ref, mxa_ref)
        absorb(t + 1, t, sb_ref, mxb_ref)

    @pl.when((rest & 1) == 0)
    def _():
        absorb(qt, block_of(qt), sa_ref, mxa_ref)

    oT = acc_ref[:head_dim, :] / acc_ref[head_dim:head_dim + 1, :]
    o_hd = jnp.concatenate([oT[:, h * blk:(h + 1) * blk] for h in range(group)], axis=0)
    o_ref[...] = o_hd.T.astype(BF16)


def _moba_attention(qT, kn, vT, kbar, *, n_kv_heads, head_dim, blk, topk, ride=()):
    nq, seq = qT.shape
    group = nq // (n_kv_heads * head_dim)
    width = group * blk
    nb = seq // blk
    kv_dim = n_kv_heads * head_dim
    grid = (nb,)
    rider = _RideAlong(ride, grid, lambda t: t)
    vmem = (2 * seq * kv_dim * 2 + 4 * nq * blk * 2
            + head_dim * width * (2 + 4) + 8 * blk * width * 4 + rider.vmem)
    return pl.pallas_call(
        functools.partial(_moba_kernel, n_kv_heads=n_kv_heads, group=group, head_dim=head_dim,
                          blk=blk, topk=topk, n_ride=len(rider.args)),
        grid=grid,
        in_specs=[pl.BlockSpec((nq, blk), lambda t: (0, t)),
                  _resident((seq, kv_dim)),
                  _resident((kv_dim, seq)),
                  _resident((nb, kv_dim))] + rider.in_specs,
        out_specs=[pl.BlockSpec((blk, nq), lambda t: (t, 0))] + rider.out_specs,
        out_shape=[jax.ShapeDtypeStruct((seq, nq), BF16)] + rider.out_shape,
        scratch_shapes=[pltpu.VMEM((head_dim, width), BF16),
                        pltpu.VMEM((nb + SUBLANES, width), F32),
                        pltpu.VMEM((blk, width), F32),
                        pltpu.VMEM((blk, width), F32),
                        pltpu.VMEM((1, width), F32),
                        pltpu.VMEM((1, width), F32),
                        pltpu.VMEM((1, width), F32),
                        pltpu.VMEM((head_dim + BF16_SUBLANES, width), F32)],
        compiler_params=_params(("parallel",), vmem, claim_all=True),
        name="moba_attention",
    )(qT, kn, vT, kbar, *rider.args)


def _oproj_ln_kernel(o_ref, w_ref, x_ref, g_ref, b_ref, h_ref, *, row_chunks=4):
    rows = o_ref.shape[0] // row_chunks
    for r in range(row_chunks):
        sl = slice(r * rows, (r + 1) * rows)
        y = jnp.dot(o_ref[sl, :], w_ref[...], preferred_element_type=F32)
        h_ref[sl, :] = _layer_norm(DEEPNORM_ALPHA * x_ref[sl, :] + y, g_ref[...], b_ref[...])


def _oproj_ln(o, w, x, g_row, b_row, tm=512):
    seq, d_model = x.shape
    k_dim = o.shape[1]
    vmem = (2 * tm * k_dim * 2 + k_dim * d_model * 2 + 4 * tm * d_model * 4
            + 3 * tm * d_model * 4)
    return pl.pallas_call(
        _oproj_ln_kernel,
        grid=(seq // tm,),
        in_specs=[pl.BlockSpec((tm, k_dim), lambda i: (i, 0)),
                  _resident((k_dim, d_model)),
                  pl.BlockSpec((tm, d_model), lambda i: (i, 0)),
                  _resident((1, d_model)),
                  _resident((1, d_model))],
        out_specs=pl.BlockSpec((tm, d_model), lambda i: (i, 0)),
        out_shape=jax.ShapeDtypeStruct((seq, d_model), F32),
        compiler_params=_params(("parallel",), vmem),
        name="oproj_ln",
    )(o, w, x, g_row, b_row)


def _mlp_ln_kernel(*refs, n_ride, row_chunks=2):
    ins, (out_ref,), (hb_ref, acc_ref) = _RideAlong.split(refs, 5, 1, n_ride)
    h_ref, win_ref, wout_ref, g_ref, b_ref = ins
    f = pl.program_id(1)
    last = pl.num_programs(1) - 1

    @pl.when(f == 0)
    def _():
        hb_ref[...] = h_ref[...].astype(BF16)
        acc_ref[...] = jnp.zeros_like(acc_ref)

    def hidden(rows):
        a = jnp.dot(hb_ref[rows, :], win_ref[...], preferred_element_type=F32)
        a = jnp.maximum(a, 0.0)
        a = a * a
        return jnp.dot(a.astype(BF16), wout_ref[...], preferred_element_type=F32)

    @pl.when(f < last)
    def _():
        acc_ref[...] += hidden(slice(None))

    @pl.when(f == last)
    def _():
        rows = hb_ref.shape[0] // row_chunks
        for r in range(row_chunks):
            sl = slice(r * rows, (r + 1) * rows)
            y = acc_ref[sl, :] + hidden(sl)
            out_ref[sl, :] = _layer_norm(DEEPNORM_ALPHA * h_ref[sl, :] + y,
                                         g_ref[...], b_ref[...])


def _mlp_ln(h, w_in, w_out, g_row, b_row, ride=(), tm=512, tf=1024):
    seq, d_model = h.shape
    d_ff = w_in.shape[1]
    grid = (seq // tm, d_ff // tf)
    step_of = lambda i, f: i * grid[1] + f
    rider = _RideAlong(ride, grid, step_of)
    vmem = (4 * tm * d_model * 4 + 4 * d_model * tf * 2 + tm * d_model * (2 + 4)
            + 2 * tm * tf * 4 + 2 * tm * d_model * 4 + rider.vmem)
    return pl.pallas_call(
        functools.partial(_mlp_ln_kernel, n_ride=len(rider.args)),
        grid=grid,
        in_specs=[pl.BlockSpec((tm, d_model), lambda i, f: (i, 0)),
                  pl.BlockSpec((d_model, tf), lambda i, f: (0, f)),
                  pl.BlockSpec((tf, d_model), lambda i, f: (f, 0)),
                  _resident((1, d_model)),
                  _resident((1, d_model))] + rider.in_specs,
        out_specs=[pl.BlockSpec((tm, d_model), lambda i, f: (i, 0))] + rider.out_specs,
        out_shape=[jax.ShapeDtypeStruct((seq, d_model), F32)] + rider.out_shape,
        scratch_shapes=[pltpu.VMEM((tm, d_model), BF16), pltpu.VMEM((tm, d_model), F32)],
        compiler_params=_params(("parallel", "arbitrary"), vmem),
        name="mlp_ln",
    )(h, w_in, w_out, g_row, b_row, *rider.args)


def _round_kernel(w_ref, o_ref):
    o_ref[...] = w_ref[...].astype(BF16)


def _round_layer(w_all, layer, slab=256):
    _, rows, cols = w_all.shape
    return pl.pallas_call(
        _round_kernel,
        grid=(rows // slab,),
        in_specs=[pl.BlockSpec((None, slab, cols), lambda i: (layer, i, 0))],
        out_specs=pl.BlockSpec((slab, cols), lambda i: (i, 0)),
        out_shape=jax.ShapeDtypeStruct((rows, cols), BF16),
        compiler_params=_params(("parallel",), 0, claim_all=True),
        name="round_layer",
    )(w_all)


def kernel(x, positions, a_w_qkv, a_b_qkv, a_sinks, a_w_o, b_w_q, b_w_o, b_w_kv,
           ln_attn_g, ln_attn_b, ln_mlp_g, ln_mlp_b, mlp_w_in, mlp_w_out):
    batch, seq, d_model = x.shape
    assert batch == 1 and a_w_qkv.shape[0] == 1 and b_w_q.shape[0] == 1
    h = x.reshape(seq, d_model)
    pos_row = positions.reshape(1, seq)
    row = lambda v: v.reshape(1, -1)


    cos_a, sin_a = _rope_tables(pos_row, A_HEAD_DIM)
    n_a_heads = a_sinks.shape[1]
    qT, kn, vT, w_in0 = _project(
        h, [_round_layer(a_w_qkv, 0)], a_b_qkv, cos_a, sin_a,
        n_q_heads=n_a_heads, n_kv_heads=A_KV_HEADS, head_dim=A_HEAD_DIM,
        q_scale=A_HEAD_DIM ** -0.5 * LOG2_E, ride=[(mlp_w_in, 0)])
    sink_row = jnp.repeat(a_sinks[0].astype(F32), A_WINDOW)[None, :]
    o, w_out0, w_o0 = _swa_attention(
        qT, kn, vT, sink_row, n_kv_heads=A_KV_HEADS, head_dim=A_HEAD_DIM, blk=A_WINDOW,
        ride=[(mlp_w_out, 0), (a_w_o, 0)])
    h = _oproj_ln(o, w_o0, h, row(ln_attn_g[0]), row(ln_attn_b[0]))
    h, w_q1, w_kv1 = _mlp_ln(h, w_in0, w_out0, row(ln_mlp_g[0]), row(ln_mlp_b[0]),
                             ride=[(b_w_q, 0), (b_w_kv[None], 0)])

    cos_b, sin_b = _rope_tables(pos_row, B_HEAD_DIM)
    n_b_heads = b_w_q.shape[2] // B_HEAD_DIM
    qT, kn, vT, kbar, w_o1 = _project(
        h, [w_q1, w_kv1], None,
        cos_b, sin_b, n_q_heads=n_b_heads, n_kv_heads=B_KV_HEADS, head_dim=B_HEAD_DIM,
        q_scale=B_HEAD_DIM ** -0.5 * LOG2_E, kbar_block=MOBA_BLOCK, ride=[(b_w_o, 0)])
    kbar = kbar.reshape(seq // MOBA_BLOCK, B_KV_HEADS * B_HEAD_DIM)
    o, w_in1, w_out1 = _moba_attention(
        qT, kn, vT, kbar, n_kv_heads=B_KV_HEADS, head_dim=B_HEAD_DIM, blk=MOBA_BLOCK,
        topk=MOBA_TOPK, ride=[(mlp_w_in, 1), (mlp_w_out, 1)])
    h = _oproj_ln(o, w_o1, h, row(ln_attn_g[1]), row(ln_attn_b[1]))
    (h,) = _mlp_ln(h, w_in1, w_out1, row(ln_mlp_g[1]), row(ln_mlp_b[1]))
    return h.reshape(batch, seq, d_model)
```

```python
import functools
import math

import jax
import jax.numpy as jnp
from jax import lax
from jax.experimental import pallas as pl
from jax.experimental.pallas import tpu as pltpu

A_HEAD_DIM = 64
A_KV_HEADS = 4
A_WINDOW = 128
B_HEAD_DIM = 128
B_KV_HEADS = 4
MOBA_BLOCK = 256
MOBA_TOPK = 3
ROPE_THETA = 10000.0
LN_EPS = 1e-5
DEPTH = 2
DEEPNORM_ALPHA = (2 * DEPTH) ** 0.25
LOG2_E = math.log2(math.e)

V7X_VMEM_BYTES = 64 * 1024 * 1024
V7X_SCOPED_VMEM_BYTES = 60 * 1024 * 1024
SUBLANES = 8
BF16_SUBLANES = 16

BF16 = jnp.bfloat16
F32 = jnp.float32
MASKED = -1e30


def _resident(shape):
    return pl.BlockSpec(shape, lambda *_: (0,) * len(shape), pipeline_mode=pl.Buffered(1))


def _params(semantics, vmem_bytes, claim_all=False):
    limit = V7X_SCOPED_VMEM_BYTES if claim_all else min(int(vmem_bytes), V7X_VMEM_BYTES)
    return pltpu.CompilerParams(dimension_semantics=semantics, vmem_limit_bytes=limit)


class _RideAlong:
    def __init__(self, weights, grid, step_of):
        steps = 1
        for n in grid:
            steps *= n
        self.args = [w for w, _ in weights]
        self.in_specs, self.out_specs, self.out_shape, self.vmem = [], [], [], 0
        for w, layer in weights:
            _, rows, cols = w.shape
            slab = rows // steps
            assert slab * steps == rows and slab % BF16_SUBLANES == 0
            self.in_specs.append(pl.BlockSpec(
                (None, slab, cols), lambda *g, layer=layer: (layer, step_of(*g), 0)))
            self.out_specs.append(pl.BlockSpec((slab, cols), lambda *g: (step_of(*g), 0)))
            self.out_shape.append(jax.ShapeDtypeStruct((rows, cols), BF16))
            self.vmem += 2 * slab * cols * (4 + 2)

    @staticmethod
    def split(refs, n_in, n_out, n_ride):
        ins, rest = refs[:n_in], refs[n_in:]
        ride_in, rest = rest[:n_ride], rest[n_ride:]
        outs, rest = rest[:n_out], rest[n_out:]
        ride_out, scratch = rest[:n_ride], rest[n_ride:]
        for src, dst in zip(ride_in, ride_out):
            dst[...] = src[...].astype(BF16)
        return ins, outs, scratch


def _layer_norm(z, g, b):
    mu = jnp.mean(z, axis=-1, keepdims=True)
    zc = z - mu
    var = jnp.mean(zc * zc, axis=-1, keepdims=True)
    return zc * lax.rsqrt(var + LN_EPS) * g + b


def _rope_table_kernel(pos_ref, inv_ref, cos_ref, sin_ref):
    ang = pos_ref[...].astype(F32) * inv_ref[...]
    cos_ref[...] = jnp.cos(ang)
    sin_ref[...] = jnp.sin(ang)


def _rope_tables(pos_row, head_dim, ts=2048):
    seq = pos_row.shape[1]
    half = head_dim // 2
    inv_freq = ROPE_THETA ** (-(jnp.arange(half, dtype=F32) * 2.0) / head_dim)
    out = jax.ShapeDtypeStruct((half, seq), F32)
    return pl.pallas_call(
        _rope_table_kernel,
        grid=(seq // ts,),
        in_specs=[pl.BlockSpec((1, ts), lambda i: (0, i)),
                  pl.BlockSpec((half, 1), lambda i: (0, 0))],
        out_specs=[pl.BlockSpec((half, ts), lambda i: (0, i))] * 2,
        out_shape=[out, out],
        name="rope_tables",
    )(pos_row, inv_freq[:, None])


def _rope_rows(y, base, half, c, s):
    x1 = y[base:base + half]
    x2 = y[base + half:base + 2 * half]
    return x1 * c - x2 * s, x2 * c + x1 * s


def _proj_kernel(*refs, w_cols, has_bias, n_q_heads, n_kv_heads, head_dim, q_scale,
                 kbar_block, n_ride, chunk=512):
    n_in = 3 + has_bias + len(w_cols)
    ins, outs, _ = _RideAlong.split(refs, n_in, 4 if kbar_block else 3, n_ride)
    x_ref, cos_ref, sin_ref = ins[:3]
    b_ref = ins[3] if has_bias else None
    w_refs = ins[3 + has_bias:]
    qT_ref, kn_ref, vT_ref = outs[:3]
    half = head_dim // 2
    nq = n_q_heads * head_dim
    nk = n_kv_heads * head_dim
    xb = x_ref[...].astype(BF16)
    c = cos_ref[...]
    s = sin_ref[...]

    def project_rows(lo, hi):
        base = 0
        for w_ref, cols in zip(w_refs, w_cols):
            if lo < base + cols:
                assert hi <= base + cols
                y = jnp.dot(xb, w_ref[:, lo - base:hi - base], preferred_element_type=F32)
                if has_bias:
                    y = y + b_ref[:, lo:hi]
                return y.T
            base += cols

    for base in range(0, nq, chunk):
        y = project_rows(base, base + chunk)
        for h in range(chunk // head_dim):
            r1, r2 = _rope_rows(y, h * head_dim, half, c, s)
            lo = base + h * head_dim
            qT_ref[lo:lo + half, :] = (r1 * q_scale).astype(BF16)
            qT_ref[lo + half:lo + head_dim, :] = (r2 * q_scale).astype(BF16)
    y = project_rows(nq, nq + nk)
    k_rows = []
    for g in range(n_kv_heads):
        k_rows.extend(_rope_rows(y, g * head_dim, half, c, s))
    k_nat = jnp.concatenate(k_rows, axis=0).T
    kn_ref[...] = k_nat.astype(BF16)
    vT_ref[...] = project_rows(nq + nk, nq + 2 * nk).astype(BF16)
    if kbar_block:
        kbar_ref = outs[3]
        tm = k_nat.shape[0]
        for blk in range(tm // kbar_block):
            kbar_ref[0, blk:blk + 1, :] = jnp.mean(
                k_nat[blk * kbar_block:(blk + 1) * kbar_block], axis=0, keepdims=True)


def _project(x, ws, b_row, cos, sin, *, n_q_heads, n_kv_heads, head_dim, q_scale,
             kbar_block=0, ride=(), tm=512):
    seq, d_model = x.shape
    w_cols = tuple(w.shape[1] for w in ws)
    n_out = sum(w_cols)
    bias = [] if b_row is None else [b_row]
    nq = n_q_heads * head_dim
    nk = n_kv_heads * head_dim
    half = head_dim // 2
    out_shape = [jax.ShapeDtypeStruct((nq, seq), BF16),
                 jax.ShapeDtypeStruct((seq, nk), BF16),
                 jax.ShapeDtypeStruct((nk, seq), BF16)]
    out_specs = [pl.BlockSpec((nq, tm), lambda i: (0, i)),
                 pl.BlockSpec((tm, nk), lambda i: (i, 0)),
                 pl.BlockSpec((nk, tm), lambda i: (0, i))]
    if kbar_block:
        nb = tm // kbar_block
        out_shape.append(jax.ShapeDtypeStruct((seq // tm, nb, nk), F32))
        out_specs.append(pl.BlockSpec((1, nb, nk), lambda i: (i, 0, 0)))
    grid = (seq // tm,)
    rider = _RideAlong(ride, grid, lambda i: i)
    vmem = (2 * tm * d_model * 4 + n_out * d_model * 2 + n_out * SUBLANES * 4
            + 2 * n_out * tm * 2 + 4 * n_out * tm * 4 + rider.vmem)
    return pl.pallas_call(
        functools.partial(_proj_kernel, w_cols=w_cols, has_bias=b_row is not None,
                          n_q_heads=n_q_heads, n_kv_heads=n_kv_heads, head_dim=head_dim,
                          q_scale=q_scale, kbar_block=kbar_block, n_ride=len(rider.args)),
        grid=grid,
        in_specs=[pl.BlockSpec((tm, d_model), lambda i: (i, 0)),
                  pl.BlockSpec((half, tm), lambda i: (0, i)),
                  pl.BlockSpec((half, tm), lambda i: (0, i))]
                 + [_resident(b.shape) for b in bias]
                 + [_resident(w.shape) for w in ws] + rider.in_specs,
        out_specs=out_specs + rider.out_specs,
        out_shape=out_shape + rider.out_shape,
        compiler_params=_params(("parallel",), vmem),
        name="qkv_proj_hd%d" % head_dim,
    )(x, cos, sin, *bias, *ws, *rider.args)


def _swa_kernel(*refs, n_kv_heads, group, head_dim, blk, n_sub, n_ride):
    ins, (o_ref,), scratch = _RideAlong.split(refs, 6, 1, n_ride)
    qT_ref, kprev_ref, kcur_ref, vprev_ref, vcur_ref, sink_ref = ins
    sa_ref, sb_ref, mxa_ref, mxb_ref = scratch
    i = pl.program_id(0)
    width = group * blk
    kj = lax.broadcasted_iota(jnp.int32, (blk, width), 0)
    qi = lax.broadcasted_iota(jnp.int32, (blk, width), 1) & (blk - 1)
    in_cur = kj <= qi
    cur01 = jnp.where(in_cur, 1.0, 0.0).astype(BF16)
    no_prev_bias = jnp.where(i > 0, 0.0, MASKED)
    kv_dim = n_kv_heads * head_dim

    def score(c, g, s_ref, mx_ref):
        lo, hi = c * blk, (c + 1) * blk
        if c == 0:
            k_band = jnp.concatenate([kprev_ref[...], kcur_ref[lo:hi, :]], axis=0)
        else:
            k_band = kcur_ref[lo - blk:hi, :]
        heads = range(g * group, (g + 1) * group)
        q_g = jnp.concatenate(
            [qT_ref[h * head_dim:(h + 1) * head_dim, lo:hi] for h in heads], axis=1)
        pads = [jnp.zeros((g * head_dim, width), BF16)] if g else []
        pads.append(q_g)
        if g + 1 < n_kv_heads:
            pads.append(jnp.zeros((kv_dim - (g + 1) * head_dim, width), BF16))
        q_pad = jnp.concatenate(pads, axis=0) if len(pads) > 1 else q_g
        s_band = jnp.dot(k_band, q_pad, preferred_element_type=F32)
        s_prev = s_band[:blk] + no_prev_bias if c == 0 else s_band[:blk]
        s = jnp.where(in_cur, s_band[blk:], s_prev)
        s_ref[...] = s
        mx_ref[...] = jnp.max(s, axis=0, keepdims=True)

    def attend(c, g, s_ref, mx_ref):
        lo, hi = c * blk, (c + 1) * blk
        sink = sink_ref[:, g * width:(g + 1) * width] * LOG2_E
        m = jnp.maximum(mx_ref[...], sink)
        p = jnp.exp2(s_ref[...] - m)
        denom = jnp.sum(p, axis=0, keepdims=True) + jnp.exp2(sink - m)
        p = p.astype(BF16)
        p_cur = p * cur01
        p_band = jnp.concatenate([p - p_cur, p_cur], axis=0)
        rows = slice(g * head_dim, (g + 1) * head_dim)
        if c == 0:
            v_band = jnp.concatenate([vprev_ref[rows, :], vcur_ref[rows, lo:hi]], axis=1)
        else:
            v_band = vcur_ref[rows, lo - blk:hi]
        oT = jnp.dot(v_band, p_band, preferred_element_type=F32)
        oT = oT / denom
        o_hd = jnp.concatenate([oT[:, h * blk:(h + 1) * blk] for h in range(group)], axis=0)
        o_ref[lo:hi, g * group * head_dim:(g + 1) * group * head_dim] = o_hd.T.astype(BF16)

    todo = [(c, g) for c in range(n_sub) for g in range(n_kv_heads)]
    bufs = [(sa_ref, mxa_ref), (sb_ref, mxb_ref)]
    score(*todo[0], *bufs[0])
    for n, (c, g) in enumerate(todo):
        if n + 1 < len(todo):
            score(*todo[n + 1], *bufs[(n + 1) % 2])
        attend(c, g, *bufs[n % 2])


def _swa_attention(qT, kn, vT, sink_row, *, n_kv_heads, head_dim, blk, ride=(), tq=512):
    nq, seq = qT.shape
    kv_dim = n_kv_heads * head_dim
    group = nq // kv_dim
    n_sub = tq // blk
    prev_blk = lambda i: jnp.maximum(i * n_sub - 1, 0)
    grid = (seq // tq,)
    rider = _RideAlong(ride, grid, lambda i: i)
    vmem = (2 * (nq * tq * 2 * 2 + 2 * (tq + blk) * kv_dim * 2) + 16 * blk * group * blk * 4
            + rider.vmem)
    return pl.pallas_call(
        functools.partial(_swa_kernel, n_kv_heads=n_kv_heads, group=group,
                          head_dim=head_dim, blk=blk, n_sub=n_sub, n_ride=len(rider.args)),
        grid=grid,
        in_specs=[pl.BlockSpec((nq, tq), lambda i: (0, i)),
                  pl.BlockSpec((blk, kv_dim), lambda i: (prev_blk(i), 0)),
                  pl.BlockSpec((tq, kv_dim), lambda i: (i, 0)),
                  pl.BlockSpec((kv_dim, blk), lambda i: (0, prev_blk(i))),
                  pl.BlockSpec((kv_dim, tq), lambda i: (0, i)),
                  _resident(sink_row.shape)] + rider.in_specs,
        out_specs=[pl.BlockSpec((tq, nq), lambda i: (i, 0))] + rider.out_specs,
        out_shape=[jax.ShapeDtypeStruct((seq, nq), BF16)] + rider.out_shape,
        scratch_shapes=[pltpu.VMEM((blk, group * blk), F32),
                        pltpu.VMEM((blk, group * blk), F32),
                        pltpu.VMEM((1, group * blk), F32),
                        pltpu.VMEM((1, group * blk), F32)],
        compiler_params=_params(("parallel",), vmem, claim_all=True),
        name="swa_attention",
    )(qT, kn, kn, vT, vT, sink_row, *rider.args)


def _moba_kernel(*refs, n_kv_heads, group, head_dim, blk, topk, n_ride):
    (qT_ref, kn_ref, vT_ref, kbar_ref), (o_ref,), scratch = _RideAlong.split(refs, 4, 1, n_ride)
    for g in range(n_kv_heads):
        q_rows = slice(g * group * head_dim, (g + 1) * group * head_dim)
        kv_cols = slice(g * head_dim, (g + 1) * head_dim)
        _moba_tile(qT_ref.at[q_rows, :], kn_ref.at[:, kv_cols], vT_ref.at[kv_cols, :],
                   kbar_ref.at[:, kv_cols], o_ref.at[:, q_rows], *scratch,
                   group=group, head_dim=head_dim, blk=blk, topk=topk)


def _moba_tile(qT_ref, kn_ref, vT_ref, kbar_ref, o_ref,
               q_all_ref, bias_ref, sa_ref, sb_ref, mxa_ref, mxb_ref, m_ref, acc_ref,
               *, group, head_dim, blk, topk):
    qt = pl.program_id(0)
    width = group * blk
    nb = kbar_ref.shape[0]
    own_row = bias_ref.shape[0] - nb - 1
    ones_rows = jnp.ones((acc_ref.shape[0] - head_dim, blk), BF16)
    q_all_ref[...] = jnp.concatenate(
        [qT_ref[h * head_dim:(h + 1) * head_dim, :] for h in range(group)], axis=1)
    q_all = q_all_ref[...]

    gate = jnp.dot(kbar_ref[...].astype(BF16), q_all, preferred_element_type=F32)
    row = lax.broadcasted_iota(jnp.int32, (nb, width), 0).astype(F32)
    past = row < qt.astype(F32)
    gate = jnp.where(past, gate, -jnp.inf)
    chosen = jnp.zeros((nb, width), F32)
    for _ in range(topk):
        best = jnp.max(gate, axis=0, keepdims=True)
        idx = jnp.min(jnp.where(gate == best, row, float(nb)), axis=0, keepdims=True)
        hit = row == idx
        chosen = jnp.where(hit, 1.0, chosen)
        gate = jnp.where(hit, -jnp.inf, gate)
    bias_ref[own_row:own_row + 1, :] = jnp.zeros((1, width), F32)
    bias_ref[own_row + 1:, :] = jnp.where(jnp.logical_and(chosen > 0.0, past), 0.0, MASKED)

    def scores(block):
        start = pl.multiple_of(block * blk, blk)
        return jnp.dot(kn_ref[pl.ds(start, blk), :], q_all_ref[...], preferred_element_type=F32)

    def stash(s, s_ref, mx_ref):
        s_ref[...] = s
        mx_ref[...] = jnp.max(s, axis=0, keepdims=True)

    def absorb(t, block, s_ref, mx_ref):
        bias = bias_ref[pl.ds(own_row + t, 1), :]
        m_old = m_ref[...]
        m_new = jnp.maximum(m_old, mx_ref[...] + bias)
        alpha = jnp.exp2(m_old - m_new)
        p = jnp.exp2(s_ref[...] - (m_new - bias))
        start = pl.multiple_of(block * blk, blk)
        v_ones = jnp.concatenate([vT_ref[:, pl.ds(start, blk)], ones_rows], axis=0)
        acc_ref[...] = alpha * acc_ref[...] + jnp.dot(
            v_ones, p.astype(BF16), preferred_element_type=F32)
        m_ref[...] = m_new

    m_ref[...] = jnp.full((1, width), MASKED, F32)
    acc_ref[...] = jnp.zeros(acc_ref.shape, F32)
    kj = lax.broadcasted_iota(jnp.int32, (blk, width), 0)
    qi = lax.broadcasted_iota(jnp.int32, (blk, width), 1) & (blk - 1)
    stash(jnp.where(kj <= qi, scores(qt), MASKED), sa_ref, mxa_ref)

    def block_of(t):
        return jnp.where(t == 0, qt, t - 1)

    def absorb_pairs(t, count):
        for k in range(0, count, 2):
            stash(scores(t + k), sb_ref, mxb_ref)
            absorb(t + k, block_of(t + k), sa_ref, mxa_ref)
            stash(scores(t + k + 1), sa_ref, mxa_ref)
            absorb(t + k + 1, t + k, sb_ref, mxb_ref)

    def octet(i, carry):
        absorb_pairs(8 * i, 8)
        return carry

    n_octets = lax.shift_right_logical(qt, 3)
    rest = qt & 7
    lax.fori_loop(0, n_octets, octet, 0)

    @pl.when(rest >= 4)
    def _():
        absorb_pairs(8 * n_octets, 4)

    t_rest = 8 * n_octets + (rest & 4)
    rest = rest & 3

    @pl.when(rest >= 2)
    def _():
        absorb_pairs(t_rest, 2)

    @pl.when((rest & 1) == 1)
    def _():
        t = t_rest + (rest & 2)
        stash(scores(t), sb_ref, mxb_ref)
        absorb(t, block_of(t), sa_ref, mxa_ref)
        absorb(t + 1, t, sb_ref, mxb_ref)

    @pl.when((rest & 1) == 0)
    def _():
        absorb(qt, block_of(qt), sa_ref, mxa_ref)

    oT = acc_ref[:head_dim, :] / acc_ref[head_dim:head_dim + 1, :]
    o_hd = jnp.concatenate([oT[:, h * blk:(h + 1) * blk] for h in range(group)], axis=0)
    o_ref[...] = o_hd.T.astype(BF16)


def _moba_attention(qT, kn, vT, kbar, *, n_kv_heads, head_dim, blk, topk, ride=()):
    nq, seq = qT.shape
    group = nq // (n_kv_heads * head_dim)
    width = group * blk
    nb = seq // blk
    kv_dim = n_kv_heads * head_dim
    grid = (nb,)
    rider = _RideAlong(ride, grid, lambda t: t)
    vmem = (2 * seq * kv_dim * 2 + 4 * nq * blk * 2
            + head_dim * width * (2 + 4) + 8 * blk * width * 4 + rider.vmem)
    return pl.pallas_call(
        functools.partial(_moba_kernel, n_kv_heads=n_kv_heads, group=group, head_dim=head_dim,
                          blk=blk, topk=topk, n_ride=len(rider.args)),
        grid=grid,
        in_specs=[pl.BlockSpec((nq, blk), lambda t: (0, t)),
                  _resident((seq, kv_dim)),
                  _resident((kv_dim, seq)),
                  _resident((nb, kv_dim))] + rider.in_specs,
        out_specs=[pl.BlockSpec((blk, nq), lambda t: (t, 0))] + rider.out_specs,
        out_shape=[jax.ShapeDtypeStruct((seq, nq), BF16)] + rider.out_shape,
        scratch_shapes=[pltpu.VMEM((head_dim, width), BF16),
                        pltpu.VMEM((nb + SUBLANES, width), F32),
                        pltpu.VMEM((blk, width), F32),
                        pltpu.VMEM((blk, width), F32),
                        pltpu.VMEM((1, width), F32),
                        pltpu.VMEM((1, width), F32),
                        pltpu.VMEM((1, width), F32),
                        pltpu.VMEM((head_dim + BF16_SUBLANES, width), F32)],
        compiler_params=_params(("parallel",), vmem, claim_all=True),
        name="moba_attention",
    )(qT, kn, vT, kbar, *rider.args)


def _oproj_ln_kernel(o_ref, w_ref, x_ref, g_ref, b_ref, h_ref, *, row_chunks=4):
    rows = o_ref.shape[0] // row_chunks
    for r in range(row_chunks):
        sl = slice(r * rows, (r + 1) * rows)
        y = jnp.dot(o_ref[sl, :], w_ref[...], preferred_element_type=F32)
        h_ref[sl, :] = _layer_norm(DEEPNORM_ALPHA * x_ref[sl, :] + y, g_ref[...], b_ref[...])


def _oproj_ln(o, w, x, g_row, b_row, tm=512):
    seq, d_model = x.shape
    k_dim = o.shape[1]
    vmem = (2 * tm * k_dim * 2 + k_dim * d_model * 2 + 4 * tm * d_model * 4
            + 3 * tm * d_model * 4)
    return pl.pallas_call(
        _oproj_ln_kernel,
        grid=(seq // tm,),
        in_specs=[pl.BlockSpec((tm, k_dim), lambda i: (i, 0)),
                  _resident((k_dim, d_model)),
                  pl.BlockSpec((tm, d_model), lambda i: (i, 0)),
                  _resident((1, d_model)),
                  _resident((1, d_model))],
        out_specs=pl.BlockSpec((tm, d_model), lambda i: (i, 0)),
        out_shape=jax.ShapeDtypeStruct((seq, d_model), F32),
        compiler_params=_params(("parallel",), vmem),
        name="oproj_ln",
    )(o, w, x, g_row, b_row)


def _mlp_ln_kernel(*refs, n_ride, row_chunks=2):
    ins, (out_ref,), (hb_ref, acc_ref) = _RideAlong.split(refs, 5, 1, n_ride)
    h_ref, win_ref, wout_ref, g_ref, b_ref = ins
    f = pl.program_id(1)
    last = pl.num_programs(1) - 1

    @pl.when(f == 0)
    def _():
        hb_ref[...] = h_ref[...].astype(BF16)
        acc_ref[...] = jnp.zeros_like(acc_ref)

    def hidden(rows):
        a = jnp.dot(hb_ref[rows, :], win_ref[...], preferred_element_type=F32)
        a = jnp.maximum(a, 0.0)
        a = a * a
        return jnp.dot(a.astype(BF16), wout_ref[...], preferred_element_type=F32)

    @pl.when(f < last)
    def _():
        acc_ref[...] += hidden(slice(None))

    @pl.when(f == last)
    def _():
        rows = hb_ref.shape[0] // row_chunks
        for r in range(row_chunks):
            sl = slice(r * rows, (r + 1) * rows)
            y = acc_ref[sl, :] + hidden(sl)
            out_ref[sl, :] = _layer_norm(DEEPNORM_ALPHA * h_ref[sl, :] + y,
                                         g_ref[...], b_ref[...])


def _mlp_ln(h, w_in, w_out, g_row, b_row, ride=(), tm=512, tf=1024):
    seq, d_model = h.shape
    d_ff = w_in.shape[1]
    grid = (seq // tm, d_ff // tf)
    step_of = lambda i, f: i * grid[1] + f
    rider = _RideAlong(ride, grid, step_of)
    vmem = (4 * tm * d_model * 4 + 4 * d_model * tf * 2 + tm * d_model * (2 + 4)
            + 2 * tm * tf * 4 + 2 * tm * d_model * 4 + rider.vmem)
    return pl.pallas_call(
        functools.partial(_mlp_ln_kernel, n_ride=len(rider.args)),
        grid=grid,
        in_specs=[pl.BlockSpec((tm, d_model), lambda i, f: (i, 0)),
                  pl.BlockSpec((d_model, tf), lambda i, f: (0, f)),
                  pl.BlockSpec((tf, d_model), lambda i, f: (f, 0)),
                  _resident((1, d_model)),
                  _resident((1, d_model))] + rider.in_specs,
        out_specs=[pl.BlockSpec((tm, d_model), lambda i, f: (i, 0))] + rider.out_specs,
        out_shape=[jax.ShapeDtypeStruct((seq, d_model), F32)] + rider.out_shape,
        scratch_shapes=[pltpu.VMEM((tm, d_model), BF16), pltpu.VMEM((tm, d_model), F32)],
        compiler_params=_params(("parallel", "arbitrary"), vmem),
        name="mlp_ln",
    )(h, w_in, w_out, g_row, b_row, *rider.args)


def _round_kernel(w_ref, o_ref):
    o_ref[...] = w_ref[...].astype(BF16)


def _round_layer(w_all, layer, slab=256):
    _, rows, cols = w_all.shape
    return pl.pallas_call(
        _round_kernel,
        grid=(rows // slab,),
        in_specs=[pl.BlockSpec((None, slab, cols), lambda i: (layer, i, 0))],
        out_specs=pl.BlockSpec((slab, cols), lambda i: (i, 0)),
        out_shape=jax.ShapeDtypeStruct((rows, cols), BF16),
        compiler_params=_params(("parallel",), 0, claim_all=True),
        name="round_layer",
    )(w_all)


def kernel(x, positions, a_w_qkv, a_b_qkv, a_sinks, a_w_o, b_w_q, b_w_o, b_w_kv,
           ln_attn_g, ln_attn_b, ln_mlp_g, ln_mlp_b, mlp_w_in, mlp_w_out):
    batch, seq, d_model = x.shape
    assert batch == 1 and a_w_qkv.shape[0] == 1 and b_w_q.shape[0] == 1
    h = x.reshape(seq, d_model)
    pos_row = positions.reshape(1, seq)
    row = lambda v: v.reshape(1, -1)


    cos_a, sin_a = _rope_tables(pos_row, A_HEAD_DIM)
    n_a_heads = a_sinks.shape[1]
    qT, kn, vT, w_in0 = _project(
        h, [_round_layer(a_w_qkv, 0)], a_b_qkv, cos_a, sin_a,
        n_q_heads=n_a_heads, n_kv_heads=A_KV_HEADS, head_dim=A_HEAD_DIM,
        q_scale=A_HEAD_DIM ** -0.5 * LOG2_E, ride=[(mlp_w_in, 0)])
    sink_row = jnp.repeat(a_sinks[0].astype(F32), A_WINDOW)[None, :]
    o, w_out0, w_o0 = _swa_attention(
        qT, kn, vT, sink_row, n_kv_heads=A_KV_HEADS, head_dim=A_HEAD_DIM, blk=A_WINDOW,
        ride=[(mlp_w_out, 0), (a_w_o, 0)])
    h = _oproj_ln(o, w_o0, h, row(ln_attn_g[0]), row(ln_attn_b[0]))
    h, w_q1, w_kv1 = _mlp_ln(h, w_in0, w_out0, row(ln_mlp_g[0]), row(ln_mlp_b[0]),
                             ride=[(b_w_q, 0), (b_w_kv[None], 0)])

    cos_b, sin_b = _rope_tables(pos_row, B_HEAD_DIM)
    n_b_heads = b_w_q.shape[2] // B_HEAD_DIM
    qT, kn, vT, kbar, w_o1 = _project(
        h, [w_q1, w_kv1], None,
        cos_b, sin_b, n_q_heads=n_b_heads, n_kv_heads=B_KV_HEADS, head_dim=B_HEAD_DIM,
        q_scale=B_HEAD_DIM ** -0.5 * LOG2_E, kbar_block=MOBA_BLOCK, ride=[(b_w_o, 0)])
    kbar = kbar.reshape(seq // MOBA_BLOCK, B_KV_HEADS * B_HEAD_DIM)
    o, w_in1, w_out1 = _moba_attention(
        qT, kn, vT, kbar, n_kv_heads=B_KV_HEADS, head_dim=B_HEAD_DIM, blk=MOBA_BLOCK,
        topk=MOBA_TOPK, ride=[(mlp_w_in, 1), (mlp_w_out, 1)])
    h = _oproj_ln(o, w_o1, h, row(ln_attn_g[1]), row(ln_attn_b[1]))
    (h,) = _mlp_ln(h, w_in1, w_out1, row(ln_mlp_g[1]), row(ln_mlp_b[1]))
    return h.reshape(batch, seq, d_model)
```

```python
import functools
import math

import jax
import jax.numpy as jnp
from jax import lax
from jax.experimental import pallas as pl
from jax.experimental.pallas import tpu as pltpu

A_HEAD_DIM = 64
A_KV_HEADS = 4
A_WINDOW = 128
B_HEAD_DIM = 128
B_KV_HEADS = 4
MOBA_BLOCK = 256
MOBA_TOPK = 3
ROPE_THETA = 10000.0
LN_EPS = 1e-5
DEPTH = 2
DEEPNORM_ALPHA = (2 * DEPTH) ** 0.25
LOG2_E = math.log2(math.e)

V7X_VMEM_BYTES = 64 * 1024 * 1024
V7X_SCOPED_VMEM_BYTES = V7X_VMEM_BYTES - 4 * 1024 * 1024
SUBLANES = 8
BF16_SUBLANES = 16

BF16 = jnp.bfloat16
F32 = jnp.float32
MASKED = -1e30


def _resident(shape):
    return pl.BlockSpec(shape, lambda *_: (0,) * len(shape), pipeline_mode=pl.Buffered(1))


def _params(semantics):
    return pltpu.CompilerParams(dimension_semantics=semantics,
                                vmem_limit_bytes=V7X_SCOPED_VMEM_BYTES)


class _RideAlong:
    def __init__(self, weights, grid, step_of):
        steps = 1
        for n in grid:
            steps *= n
        self.args = [w for w, _ in weights]
        self.in_specs, self.out_specs, self.out_shape = [], [], []
        for w, layer in weights:
            _, rows, cols = w.shape
            slab = rows // steps
            assert slab * steps == rows and slab % BF16_SUBLANES == 0
            self.in_specs.append(pl.BlockSpec(
                (None, slab, cols), lambda *g, layer=layer: (layer, step_of(*g), 0)))
            self.out_specs.append(pl.BlockSpec((slab, cols), lambda *g: (step_of(*g), 0)))
            self.out_shape.append(jax.ShapeDtypeStruct((rows, cols), BF16))

    @staticmethod
    def split(refs, n_in, n_out, n_ride):
        ins, rest = refs[:n_in], refs[n_in:]
        ride_in, rest = rest[:n_ride], rest[n_ride:]
        outs, rest = rest[:n_out], rest[n_out:]
        ride_out, scratch = rest[:n_ride], rest[n_ride:]
        for src, dst in zip(ride_in, ride_out):
            dst[...] = src[...].astype(BF16)
        return ins, outs, scratch


def _layer_norm(z, g, b):
    mu = jnp.mean(z, axis=-1, keepdims=True)
    zc = z - mu
    var = jnp.mean(zc * zc, axis=-1, keepdims=True)
    return zc * lax.rsqrt(var + LN_EPS) * g + b


def _rope_table_kernel(pos_ref, inv_ref, cos_ref, sin_ref):
    ang = pos_ref[...].astype(F32) * inv_ref[...]
    cos_ref[...] = jnp.cos(ang)
    sin_ref[...] = jnp.sin(ang)


def _rope_tables(pos_row, head_dim, ts=2048):
    seq = pos_row.shape[1]
    half = head_dim // 2
    inv_freq = ROPE_THETA ** (-(jnp.arange(half, dtype=F32) * 2.0) / head_dim)
    out = jax.ShapeDtypeStruct((half, seq), F32)
    return pl.pallas_call(
        _rope_table_kernel,
        grid=(seq // ts,),
        in_specs=[pl.BlockSpec((1, ts), lambda i: (0, i)),
                  pl.BlockSpec((half, 1), lambda i: (0, 0))],
        out_specs=[pl.BlockSpec((half, ts), lambda i: (0, i))] * 2,
        out_shape=[out, out],
        compiler_params=_params(("parallel",)),
        name="rope_tables",
    )(pos_row, inv_freq[:, None])


def _rope_rows(y, base, half, c, s):
    x1 = y[base:base + half]
    x2 = y[base + half:base + 2 * half]
    return x1 * c - x2 * s, x2 * c + x1 * s


def _proj_kernel(*refs, w_cols, has_bias, n_q_heads, n_kv_heads, head_dim, q_scale,
                 kbar_block, n_ride, chunk=512):
    n_in = 3 + has_bias + len(w_cols)
    ins, outs, _ = _RideAlong.split(refs, n_in, 4 if kbar_block else 3, n_ride)
    x_ref, cos_ref, sin_ref = ins[:3]
    b_ref = ins[3] if has_bias else None
    w_refs = ins[3 + has_bias:]
    qT_ref, kn_ref, vT_ref = outs[:3]
    half = head_dim // 2
    nq = n_q_heads * head_dim
    nk = n_kv_heads * head_dim
    xb = x_ref[...].astype(BF16)
    c = cos_ref[...]
    s = sin_ref[...]

    def project_rows(lo, hi):
        base = 0
        for w_ref, cols in zip(w_refs, w_cols):
            if lo < base + cols:
                assert hi <= base + cols
                y = jnp.dot(xb, w_ref[:, lo - base:hi - base], preferred_element_type=F32)
                if has_bias:
                    y = y + b_ref[:, lo:hi]
                return y.T
            base += cols

    for base in range(0, nq, chunk):
        y = project_rows(base, base + chunk)
        for h in range(chunk // head_dim):
            r1, r2 = _rope_rows(y, h * head_dim, half, c, s)
            lo = base + h * head_dim
            qT_ref[lo:lo + half, :] = (r1 * q_scale).astype(BF16)
            qT_ref[lo + half:lo + head_dim, :] = (r2 * q_scale).astype(BF16)
    y = project_rows(nq, nq + nk)
    k_rows = []
    for g in range(n_kv_heads):
        k_rows.extend(_rope_rows(y, g * head_dim, half, c, s))
    k_nat = jnp.concatenate(k_rows, axis=0).T
    kn_ref[...] = k_nat.astype(BF16)
    vT_ref[...] = project_rows(nq + nk, nq + 2 * nk).astype(BF16)
    if kbar_block:
        kbar_ref = outs[3]
        tm = k_nat.shape[0]
        for blk in range(tm // kbar_block):
            kbar_ref[0, blk:blk + 1, :] = jnp.mean(
                k_nat[blk * kbar_block:(blk + 1) * kbar_block], axis=0, keepdims=True)


def _project(x, ws, b_row, cos, sin, *, n_q_heads, n_kv_heads, head_dim, q_scale,
             kbar_block=0, ride=(), tm=512):
    seq, d_model = x.shape
    w_cols = tuple(w.shape[1] for w in ws)
    n_out = sum(w_cols)
    bias = [] if b_row is None else [b_row]
    nq = n_q_heads * head_dim
    nk = n_kv_heads * head_dim
    half = head_dim // 2
    out_shape = [jax.ShapeDtypeStruct((nq, seq), BF16),
                 jax.ShapeDtypeStruct((seq, nk), BF16),
                 jax.ShapeDtypeStruct((nk, seq), BF16)]
    out_specs = [pl.BlockSpec((nq, tm), lambda i: (0, i)),
                 pl.BlockSpec((tm, nk), lambda i: (i, 0)),
                 pl.BlockSpec((nk, tm), lambda i: (0, i))]
    if kbar_block:
        nb = tm // kbar_block
        out_shape.append(jax.ShapeDtypeStruct((seq // tm, nb, nk), F32))
        out_specs.append(pl.BlockSpec((1, nb, nk), lambda i: (i, 0, 0)))
    grid = (seq // tm,)
    rider = _RideAlong(ride, grid, lambda i: i)
    return pl.pallas_call(
        functools.partial(_proj_kernel, w_cols=w_cols, has_bias=b_row is not None,
                          n_q_heads=n_q_heads, n_kv_heads=n_kv_heads, head_dim=head_dim,
                          q_scale=q_scale, kbar_block=kbar_block, n_ride=len(rider.args)),
        grid=grid,
        in_specs=[pl.BlockSpec((tm, d_model), lambda i: (i, 0)),
                  pl.BlockSpec((half, tm), lambda i: (0, i)),
                  pl.BlockSpec((half, tm), lambda i: (0, i))]
                 + [_resident(b.shape) for b in bias]
                 + [_resident(w.shape) for w in ws] + rider.in_specs,
        out_specs=out_specs + rider.out_specs,
        out_shape=out_shape + rider.out_shape,
        compiler_params=_params(("parallel",)),
        name="qkv_proj_hd%d" % head_dim,
    )(x, cos, sin, *bias, *ws, *rider.args)


def _swa_kernel(*refs, n_kv_heads, group, head_dim, blk, n_sub, n_ride):
    ins, (o_ref,), scratch = _RideAlong.split(refs, 6, 1, n_ride)
    qT_ref, kprev_ref, kcur_ref, vprev_ref, vcur_ref, sink_ref = ins
    sa_ref, sb_ref, mxa_ref, mxb_ref = scratch
    i = pl.program_id(0)
    width = group * blk
    kj = lax.broadcasted_iota(jnp.int32, (blk, width), 0)
    qi = lax.broadcasted_iota(jnp.int32, (blk, width), 1) & (blk - 1)
    in_cur = kj <= qi
    cur01 = jnp.where(in_cur, 1.0, 0.0).astype(BF16)
    no_prev_bias = jnp.where(i > 0, 0.0, MASKED)
    kv_dim = n_kv_heads * head_dim

    def score(c, g, s_ref, mx_ref):
        lo, hi = c * blk, (c + 1) * blk
        if c == 0:
            k_band = jnp.concatenate([kprev_ref[...], kcur_ref[lo:hi, :]], axis=0)
        else:
            k_band = kcur_ref[lo - blk:hi, :]
        heads = range(g * group, (g + 1) * group)
        q_g = jnp.concatenate(
            [qT_ref[h * head_dim:(h + 1) * head_dim, lo:hi] for h in heads], axis=1)
        pads = [jnp.zeros((g * head_dim, width), BF16)] if g else []
        pads.append(q_g)
        if g + 1 < n_kv_heads:
            pads.append(jnp.zeros((kv_dim - (g + 1) * head_dim, width), BF16))
        q_pad = jnp.concatenate(pads, axis=0) if len(pads) > 1 else q_g
        s_band = jnp.dot(k_band, q_pad, preferred_element_type=F32)
        s_prev = s_band[:blk] + no_prev_bias if c == 0 else s_band[:blk]
        s = jnp.where(in_cur, s_band[blk:], s_prev)
        s_ref[...] = s
        mx_ref[...] = jnp.max(s, axis=0, keepdims=True)

    def attend(c, g, s_ref, mx_ref):
        lo, hi = c * blk, (c + 1) * blk
        sink = sink_ref[:, g * width:(g + 1) * width] * LOG2_E
        m = jnp.maximum(mx_ref[...], sink)
        p = jnp.exp2(s_ref[...] - m)
        denom = jnp.sum(p, axis=0, keepdims=True) + jnp.exp2(sink - m)
        p = p.astype(BF16)
        p_cur = p * cur01
        p_band = jnp.concatenate([p - p_cur, p_cur], axis=0)
        rows = slice(g * head_dim, (g + 1) * head_dim)
        if c == 0:
            v_band = jnp.concatenate([vprev_ref[rows, :], vcur_ref[rows, lo:hi]], axis=1)
        else:
            v_band = vcur_ref[rows, lo - blk:hi]
        oT = jnp.dot(v_band, p_band, preferred_element_type=F32)
        oT = oT / denom
        o_hd = jnp.concatenate([oT[:, h * blk:(h + 1) * blk] for h in range(group)], axis=0)
        o_ref[lo:hi, g * group * head_dim:(g + 1) * group * head_dim] = o_hd.T.astype(BF16)

    todo = [(c, g) for c in range(n_sub) for g in range(n_kv_heads)]
    bufs = [(sa_ref, mxa_ref), (sb_ref, mxb_ref)]
    score(*todo[0], *bufs[0])
    for n, (c, g) in enumerate(todo):
        if n + 1 < len(todo):
            score(*todo[n + 1], *bufs[(n + 1) % 2])
        attend(c, g, *bufs[n % 2])


def _swa_attention(qT, kn, vT, sink_row, *, n_kv_heads, head_dim, blk, ride=(), tq=512):
    nq, seq = qT.shape
    kv_dim = n_kv_heads * head_dim
    group = nq // kv_dim
    n_sub = tq // blk
    prev_blk = lambda i: jnp.maximum(i * n_sub - 1, 0)
    grid = (seq // tq,)
    rider = _RideAlong(ride, grid, lambda i: i)
    return pl.pallas_call(
        functools.partial(_swa_kernel, n_kv_heads=n_kv_heads, group=group,
                          head_dim=head_dim, blk=blk, n_sub=n_sub, n_ride=len(rider.args)),
        grid=grid,
        in_specs=[pl.BlockSpec((nq, tq), lambda i: (0, i)),
                  pl.BlockSpec((blk, kv_dim), lambda i: (prev_blk(i), 0)),
                  pl.BlockSpec((tq, kv_dim), lambda i: (i, 0)),
                  pl.BlockSpec((kv_dim, blk), lambda i: (0, prev_blk(i))),
                  pl.BlockSpec((kv_dim, tq), lambda i: (0, i)),
                  _resident(sink_row.shape)] + rider.in_specs,
        out_specs=[pl.BlockSpec((tq, nq), lambda i: (i, 0))] + rider.out_specs,
        out_shape=[jax.ShapeDtypeStruct((seq, nq), BF16)] + rider.out_shape,
        scratch_shapes=[pltpu.VMEM((blk, group * blk), F32),
                        pltpu.VMEM((blk, group * blk), F32),
                        pltpu.VMEM((1, group * blk), F32),
                        pltpu.VMEM((1, group * blk), F32)],
        compiler_params=_params(("parallel",)),
        name="swa_attention",
    )(qT, kn, kn, vT, vT, sink_row, *rider.args)


def _moba_kernel(*refs, n_kv_heads, group, head_dim, blk, topk, n_ride):
    (qT_ref, kn_ref, vT_ref, kbar_ref), (o_ref,), scratch = _RideAlong.split(refs, 4, 1, n_ride)
    for g in range(n_kv_heads):
        q_rows = slice(g * group * head_dim, (g + 1) * group * head_dim)
        kv_cols = slice(g * head_dim, (g + 1) * head_dim)
        _moba_tile(qT_ref.at[q_rows, :], kn_ref.at[:, kv_cols], vT_ref.at[kv_cols, :],
                   kbar_ref.at[:, kv_cols], o_ref.at[:, q_rows], *scratch,
                   group=group, head_dim=head_dim, blk=blk, topk=topk)


def _moba_tile(qT_ref, kn_ref, vT_ref, kbar_ref, o_ref,
               q_all_ref, bias_ref, sa_ref, sb_ref, mxa_ref, mxb_ref, m_ref, acc_ref,
               *, group, head_dim, blk, topk):
    qt = pl.program_id(0)
    width = group * blk
    nb = kbar_ref.shape[0]
    own_row = bias_ref.shape[0] - nb - 1
    ones_rows = jnp.ones((acc_ref.shape[0] - head_dim, blk), BF16)
    q_all_ref[...] = jnp.concatenate(
        [qT_ref[h * head_dim:(h + 1) * head_dim, :] for h in range(group)], axis=1)
    q_all = q_all_ref[...]

    gate = jnp.dot(kbar_ref[...].astype(BF16), q_all, preferred_element_type=F32)
    row = lax.broadcasted_iota(jnp.int32, (nb, width), 0).astype(F32)
    past = row < qt.astype(F32)
    gate = jnp.where(past, gate, -jnp.inf)
    chosen = jnp.zeros((nb, width), F32)
    for _ in range(topk):
        best = jnp.max(gate, axis=0, keepdims=True)
        idx = jnp.min(jnp.where(gate == best, row, float(nb)), axis=0, keepdims=True)
        hit = row == idx
        chosen = jnp.where(hit, 1.0, chosen)
        gate = jnp.where(hit, -jnp.inf, gate)
    bias_ref[own_row:own_row + 1, :] = jnp.zeros((1, width), F32)
    bias_ref[own_row + 1:, :] = jnp.where(jnp.logical_and(chosen > 0.0, past), 0.0, MASKED)

    def scores(block):
        start = pl.multiple_of(block * blk, blk)
        return jnp.dot(kn_ref[pl.ds(start, blk), :], q_all_ref[...], preferred_element_type=F32)

    def stash(s, s_ref, mx_ref):
        s_ref[...] = s
        mx_ref[...] = jnp.max(s, axis=0, keepdims=True)

    def absorb(t, block, s_ref, mx_ref):
        bias = bias_ref[pl.ds(own_row + t, 1), :]
        m_old = m_ref[...]
        m_new = jnp.maximum(m_old, mx_ref[...] + bias)
        alpha = jnp.exp2(m_old - m_new)
        p = jnp.exp2(s_ref[...] - (m_new - bias))
        start = pl.multiple_of(block * blk, blk)
        v_ones = jnp.concatenate([vT_ref[:, pl.ds(start, blk)], ones_rows], axis=0)
        acc_ref[...] = alpha * acc_ref[...] + jnp.dot(
            v_ones, p.astype(BF16), preferred_element_type=F32)
        m_ref[...] = m_new

    m_ref[...] = jnp.full((1, width), MASKED, F32)
    acc_ref[...] = jnp.zeros(acc_ref.shape, F32)
    kj = lax.broadcasted_iota(jnp.int32, (blk, width), 0)
    qi = lax.broadcasted_iota(jnp.int32, (blk, width), 1) & (blk - 1)
    stash(jnp.where(kj <= qi, scores(qt), MASKED), sa_ref, mxa_ref)

    def block_of(t):
        return jnp.where(t == 0, qt, t - 1)

    def absorb_pairs(t, count):
        for k in range(0, count, 2):
            stash(scores(t + k), sb_ref, mxb_ref)
            absorb(t + k, block_of(t + k), sa_ref, mxa_ref)
            stash(scores(t + k + 1), sa_ref, mxa_ref)
            absorb(t + k + 1, t + k, sb_ref, mxb_ref)

    def octet(i, carry):
        absorb_pairs(8 * i, 8)
        return carry

    n_octets = lax.shift_right_logical(qt, 3)
    rest = qt & 7
    lax.fori_loop(0, n_octets, octet, 0)

    @pl.when(rest >= 4)
    def _():
        absorb_pairs(8 * n_octets, 4)

    t_rest = 8 * n_octets + (rest & 4)
    rest = rest & 3

    @pl.when(rest >= 2)
    def _():
        absorb_pairs(t_rest, 2)

    @pl.when((rest & 1) == 1)
    def _():
        t = t_rest + (rest & 2)
        stash(scores(t), sb_ref, mxb_ref)
        absorb(t, block_of(t), sa_ref, mxa_ref)
        absorb(t + 1, t, sb_ref, mxb_ref)

    @pl.when((rest & 1) == 0)
    def _():
        absorb(qt, block_of(qt), sa_ref, mxa_ref)

    oT = acc_ref[:head_dim, :] / acc_ref[head_dim:head_dim + 1, :]
    o_hd = jnp.concatenate([oT[:, h * blk:(h + 1) * blk] for h in range(group)], axis=0)
    o_ref[...] = o_hd.T.astype(BF16)


def _moba_attention(qT, kn, vT, kbar, *, n_kv_heads, head_dim, blk, topk, ride=()):
    nq, seq = qT.shape
    group = nq // (n_kv_heads * head_dim)
    width = group * blk
    nb = seq // blk
    kv_dim = n_kv_heads * head_dim
    grid = (nb,)
    rider = _RideAlong(ride, grid, lambda t: t)
    return pl.pallas_call(
        functools.partial(_moba_kernel, n_kv_heads=n_kv_heads, group=group, head_dim=head_dim,
                          blk=blk, topk=topk, n_ride=len(rider.args)),
        grid=grid,
        in_specs=[pl.BlockSpec((nq, blk), lambda t: (0, t)),
                  _resident((seq, kv_dim)),
                  _resident((kv_dim, seq)),
                  _resident((nb, kv_dim))] + rider.in_specs,
        out_specs=[pl.BlockSpec((blk, nq), lambda t: (t, 0))] + rider.out_specs,
        out_shape=[jax.ShapeDtypeStruct((seq, nq), BF16)] + rider.out_shape,
        scratch_shapes=[pltpu.VMEM((head_dim, width), BF16),
                        pltpu.VMEM((nb + SUBLANES, width), F32),
                        pltpu.VMEM((blk, width), F32),
                        pltpu.VMEM((blk, width), F32),
                        pltpu.VMEM((1, width), F32),
                        pltpu.VMEM((1, width), F32),
                        pltpu.VMEM((1, width), F32),
                        pltpu.VMEM((head_dim + BF16_SUBLANES, width), F32)],
        compiler_params=_params(("parallel",)),
        name="moba_attention",
    )(qT, kn, vT, kbar, *rider.args)


def _oproj_ln_kernel(o_ref, w_ref, x_ref, g_ref, b_ref, h_ref, *, row_chunks=4):
    rows = o_ref.shape[0] // row_chunks
    for r in range(row_chunks):
        sl = slice(r * rows, (r + 1) * rows)
        y = jnp.dot(o_ref[sl, :], w_ref[...], preferred_element_type=F32)
        h_ref[sl, :] = _layer_norm(DEEPNORM_ALPHA * x_ref[sl, :] + y, g_ref[...], b_ref[...])


def _oproj_ln(o, w, x, g_row, b_row, tm=512):
    seq, d_model = x.shape
    k_dim = o.shape[1]
    return pl.pallas_call(
        _oproj_ln_kernel,
        grid=(seq // tm,),
        in_specs=[pl.BlockSpec((tm, k_dim), lambda i: (i, 0)),
                  _resident((k_dim, d_model)),
                  pl.BlockSpec((tm, d_model), lambda i: (i, 0)),
                  _resident((1, d_model)),
                  _resident((1, d_model))],
        out_specs=pl.BlockSpec((tm, d_model), lambda i: (i, 0)),
        out_shape=jax.ShapeDtypeStruct((seq, d_model), F32),
        compiler_params=_params(("parallel",)),
        name="oproj_ln",
    )(o, w, x, g_row, b_row)


def _mlp_ln_kernel(*refs, n_ride, row_chunks=2):
    ins, (out_ref,), (hb_ref, acc_ref) = _RideAlong.split(refs, 5, 1, n_ride)
    h_ref, win_ref, wout_ref, g_ref, b_ref = ins
    f = pl.program_id(1)
    last = pl.num_programs(1) - 1

    def hidden(rows):
        a = jnp.dot(hb_ref[rows, :], win_ref[...], preferred_element_type=F32)
        a = jnp.maximum(a, 0.0)
        a = a * a
        return jnp.dot(a.astype(BF16), wout_ref[...], preferred_element_type=F32)

    @pl.when(f == 0)
    def _():
        hb_ref[...] = h_ref[...].astype(BF16)
        acc_ref[...] = hidden(slice(None))

    @pl.when(jnp.logical_and(f > 0, f < last))
    def _():
        acc_ref[...] += hidden(slice(None))

    @pl.when(f == last)
    def _():
        rows = hb_ref.shape[0] // row_chunks
        for r in range(row_chunks):
            sl = slice(r * rows, (r + 1) * rows)
            y = acc_ref[sl, :] + hidden(sl)
            out_ref[sl, :] = _layer_norm(DEEPNORM_ALPHA * h_ref[sl, :] + y,
                                         g_ref[...], b_ref[...])


def _mlp_ln(h, w_in, w_out, g_row, b_row, ride=(), tm=512, tf=1024):
    seq, d_model = h.shape
    d_ff = w_in.shape[1]
    grid = (seq // tm, d_ff // tf)
    assert grid[1] >= 2
    step_of = lambda i, f: i * grid[1] + f
    rider = _RideAlong(ride, grid, step_of)
    return pl.pallas_call(
        functools.partial(_mlp_ln_kernel, n_ride=len(rider.args)),
        grid=grid,
        in_specs=[pl.BlockSpec((tm, d_model), lambda i, f: (i, 0)),
                  pl.BlockSpec((d_model, tf), lambda i, f: (0, f)),
                  pl.BlockSpec((tf, d_model), lambda i, f: (f, 0)),
                  _resident((1, d_model)),
                  _resident((1, d_model))] + rider.in_specs,
        out_specs=[pl.BlockSpec((tm, d_model), lambda i, f: (i, 0))] + rider.out_specs,
        out_shape=[jax.ShapeDtypeStruct((seq, d_model), F32)] + rider.out_shape,
        scratch_shapes=[pltpu.VMEM((tm, d_model), BF16), pltpu.VMEM((tm, d_model), F32)],
        compiler_params=_params(("parallel", "arbitrary")),
        name="mlp_ln",
    )(h, w_in, w_out, g_row, b_row, *rider.args)


def _round_kernel(w_ref, o_ref):
    o_ref[...] = w_ref[...].astype(BF16)


def _round_layer(w_all, layer, slab=256):
    _, rows, cols = w_all.shape
    return pl.pallas_call(
        _round_kernel,
        grid=(rows // slab,),
        in_specs=[pl.BlockSpec((None, slab, cols), lambda i: (layer, i, 0))],
        out_specs=pl.BlockSpec((slab, cols), lambda i: (i, 0)),
        out_shape=jax.ShapeDtypeStruct((rows, cols), BF16),
        compiler_params=_params(("parallel",)),
        name="round_layer",
    )(w_all)


def kernel(x, positions, a_w_qkv, a_b_qkv, a_sinks, a_w_o, b_w_q, b_w_o, b_w_kv,
           ln_attn_g, ln_attn_b, ln_mlp_g, ln_mlp_b, mlp_w_in, mlp_w_out):
    batch, seq, d_model = x.shape
    assert batch == 1 and a_w_qkv.shape[0] == 1 and b_w_q.shape[0] == 1
    h = x.reshape(seq, d_model)
    pos_row = positions.reshape(1, seq)
    row = lambda v: v.reshape(1, -1)


    cos_a, sin_a = _rope_tables(pos_row, A_HEAD_DIM)
    n_a_heads = a_sinks.shape[1]
    qT, kn, vT, w_in0 = _project(
        h, [_round_layer(a_w_qkv, 0)], a_b_qkv, cos_a, sin_a,
        n_q_heads=n_a_heads, n_kv_heads=A_KV_HEADS, head_dim=A_HEAD_DIM,
        q_scale=A_HEAD_DIM ** -0.5 * LOG2_E, ride=[(mlp_w_in, 0)])
    sink_row = jnp.repeat(a_sinks[0].astype(F32), A_WINDOW)[None, :]
    o, w_out0, w_o0 = _swa_attention(
        qT, kn, vT, sink_row, n_kv_heads=A_KV_HEADS, head_dim=A_HEAD_DIM, blk=A_WINDOW,
        ride=[(mlp_w_out, 0), (a_w_o, 0)])
    h = _oproj_ln(o, w_o0, h, row(ln_attn_g[0]), row(ln_attn_b[0]))
    h, w_q1, w_kv1 = _mlp_ln(h, w_in0, w_out0, row(ln_mlp_g[0]), row(ln_mlp_b[0]),
                             ride=[(b_w_q, 0), (b_w_kv[None], 0)])

    cos_b, sin_b = _rope_tables(pos_row, B_HEAD_DIM)
    n_b_heads = b_w_q.shape[2] // B_HEAD_DIM
    qT, kn, vT, kbar, w_o1 = _project(
        h, [w_q1, w_kv1], None,
        cos_b, sin_b, n_q_heads=n_b_heads, n_kv_heads=B_KV_HEADS, head_dim=B_HEAD_DIM,
        q_scale=B_HEAD_DIM ** -0.5 * LOG2_E, kbar_block=MOBA_BLOCK, ride=[(b_w_o, 0)])
    kbar = kbar.reshape(seq // MOBA_BLOCK, B_KV_HEADS * B_HEAD_DIM)
    o, w_in1, w_out1 = _moba_attention(
        qT, kn, vT, kbar, n_kv_heads=B_KV_HEADS, head_dim=B_HEAD_DIM, blk=MOBA_BLOCK,
        topk=MOBA_TOPK, ride=[(mlp_w_in, 1), (mlp_w_out, 1)])
    h = _oproj_ln(o, w_o1, h, row(ln_attn_g[1]), row(ln_attn_b[1]))
    (h,) = _mlp_ln(h, w_in1, w_out1, row(ln_mlp_g[1]), row(ln_mlp_b[1]))
    return h.reshape(batch, seq, d_model)
```

```python
import functools
import math

import jax
import jax.numpy as jnp
from jax import lax
from jax.experimental import pallas as pl
from jax.experimental.pallas import tpu as pltpu

A_HEAD_DIM = 64
A_KV_HEADS = 4
A_WINDOW = 128
B_HEAD_DIM = 128
B_KV_HEADS = 4
MOBA_BLOCK = 256
MOBA_TOPK = 3
ROPE_THETA = 10000.0
LN_EPS = 1e-5
DEPTH = 2
DEEPNORM_ALPHA = (2 * DEPTH) ** 0.25
LOG2_E = math.log2(math.e)

V7X_VMEM_BYTES = 64 * 1024 * 1024
V7X_SCOPED_VMEM_BYTES = V7X_VMEM_BYTES - 4 * 1024 * 1024
SUBLANES = 8
BF16_SUBLANES = 16

BF16 = jnp.bfloat16
F32 = jnp.float32
MASKED = -1e30


def _resident(shape):
    return pl.BlockSpec(shape, lambda *_: (0,) * len(shape), pipeline_mode=pl.Buffered(1))


def _params(semantics):
    return pltpu.CompilerParams(dimension_semantics=semantics,
                                vmem_limit_bytes=V7X_SCOPED_VMEM_BYTES)


class _RideAlong:
    def __init__(self, weights, grid, step_of):
        steps = 1
        for n in grid:
            steps *= n
        self.args = [w for w, _ in weights]
        self.in_specs, self.out_specs, self.out_shape = [], [], []
        for w, layer in weights:
            _, rows, cols = w.shape
            slab = rows // steps
            assert slab * steps == rows and slab % BF16_SUBLANES == 0
            self.in_specs.append(pl.BlockSpec(
                (None, slab, cols), lambda *g, layer=layer: (layer, step_of(*g), 0)))
            self.out_specs.append(pl.BlockSpec((slab, cols), lambda *g: (step_of(*g), 0)))
            self.out_shape.append(jax.ShapeDtypeStruct((rows, cols), BF16))

    @staticmethod
    def split(refs, n_in, n_out, n_ride):
        ins, rest = refs[:n_in], refs[n_in:]
        ride_in, rest = rest[:n_ride], rest[n_ride:]
        outs, rest = rest[:n_out], rest[n_out:]
        ride_out, scratch = rest[:n_ride], rest[n_ride:]
        for src, dst in zip(ride_in, ride_out):
            dst[...] = src[...].astype(BF16)
        return ins, outs, scratch


def _layer_norm(z, g, b):
    mu = jnp.mean(z, axis=-1, keepdims=True)
    zc = z - mu
    var = jnp.mean(zc * zc, axis=-1, keepdims=True)
    return zc * lax.rsqrt(var + LN_EPS) * g + b


def _rope_rows(y, base, half, c, s):
    x1 = y[base:base + half]
    x2 = y[base + half:base + 2 * half]
    return x1 * c - x2 * s, x2 * c + x1 * s


def _proj_kernel(*refs, w_cols, has_bias, n_q_heads, n_kv_heads, head_dim, q_scale,
                 kbar_block, n_ride, chunk=512):
    n_in = 3 + has_bias + len(w_cols)
    ins, outs, _ = _RideAlong.split(refs, n_in, 4 if kbar_block else 3, n_ride)
    x_ref, pos_ref, inv_ref = ins[:3]
    b_ref = ins[3] if has_bias else None
    w_refs = ins[3 + has_bias:]
    qT_ref, kn_ref, vT_ref = outs[:3]
    half = head_dim // 2
    nq = n_q_heads * head_dim
    nk = n_kv_heads * head_dim
    xb = x_ref[...].astype(BF16)
    ang = pos_ref[...].astype(F32) * inv_ref[...]
    c = jnp.cos(ang)
    s = jnp.sin(ang)

    def project_rows(lo, hi):
        base = 0
        for w_ref, cols in zip(w_refs, w_cols):
            if lo < base + cols:
                assert hi <= base + cols
                y = jnp.dot(xb, w_ref[:, lo - base:hi - base], preferred_element_type=F32)
                if has_bias:
                    y = y + b_ref[:, lo:hi]
                return y.T
            base += cols

    for base in range(0, nq, chunk):
        y = project_rows(base, base + chunk)
        for h in range(chunk // head_dim):
            r1, r2 = _rope_rows(y, h * head_dim, half, c, s)
            lo = base + h * head_dim
            qT_ref[lo:lo + half, :] = (r1 * q_scale).astype(BF16)
            qT_ref[lo + half:lo + head_dim, :] = (r2 * q_scale).astype(BF16)
    y = project_rows(nq, nq + nk)
    k_rows = []
    for g in range(n_kv_heads):
        k_rows.extend(_rope_rows(y, g * head_dim, half, c, s))
    k_nat = jnp.concatenate(k_rows, axis=0).T
    kn_ref[...] = k_nat.astype(BF16)
    vT_ref[...] = project_rows(nq + nk, nq + 2 * nk).astype(BF16)
    if kbar_block:
        kbar_ref = outs[3]
        tm = k_nat.shape[0]
        for blk in range(tm // kbar_block):
            kbar_ref[0, blk:blk + 1, :] = jnp.mean(
                k_nat[blk * kbar_block:(blk + 1) * kbar_block], axis=0, keepdims=True)


def _project(x, ws, b_row, pos_row, *, n_q_heads, n_kv_heads, head_dim, q_scale,
             kbar_block=0, ride=(), tm=512):
    seq, d_model = x.shape
    w_cols = tuple(w.shape[1] for w in ws)
    n_out = sum(w_cols)
    bias = [] if b_row is None else [b_row]
    inv_freq = ROPE_THETA ** (-(jnp.arange(head_dim // 2, dtype=F32) * 2.0) / head_dim)
    nq = n_q_heads * head_dim
    nk = n_kv_heads * head_dim
    half = head_dim // 2
    out_shape = [jax.ShapeDtypeStruct((nq, seq), BF16),
                 jax.ShapeDtypeStruct((seq, nk), BF16),
                 jax.ShapeDtypeStruct((nk, seq), BF16)]
    out_specs = [pl.BlockSpec((nq, tm), lambda i: (0, i)),
                 pl.BlockSpec((tm, nk), lambda i: (i, 0)),
                 pl.BlockSpec((nk, tm), lambda i: (0, i))]
    if kbar_block:
        nb = tm // kbar_block
        out_shape.append(jax.ShapeDtypeStruct((seq // tm, nb, nk), F32))
        out_specs.append(pl.BlockSpec((1, nb, nk), lambda i: (i, 0, 0)))
    grid = (seq // tm,)
    rider = _RideAlong(ride, grid, lambda i: i)
    return pl.pallas_call(
        functools.partial(_proj_kernel, w_cols=w_cols, has_bias=b_row is not None,
                          n_q_heads=n_q_heads, n_kv_heads=n_kv_heads, head_dim=head_dim,
                          q_scale=q_scale, kbar_block=kbar_block, n_ride=len(rider.args)),
        grid=grid,
        in_specs=[pl.BlockSpec((tm, d_model), lambda i: (i, 0)),
                  pl.BlockSpec((1, tm), lambda i: (0, i)),
                  _resident((half, 1))]
                 + [_resident(b.shape) for b in bias]
                 + [_resident(w.shape) for w in ws] + rider.in_specs,
        out_specs=out_specs + rider.out_specs,
        out_shape=out_shape + rider.out_shape,
        compiler_params=_params(("parallel",)),
        name="qkv_proj_hd%d" % head_dim,
    )(x, pos_row, inv_freq[:, None], *bias, *ws, *rider.args)


def _swa_kernel(*refs, n_kv_heads, group, head_dim, blk, n_sub, n_ride):
    ins, (o_ref,), scratch = _RideAlong.split(refs, 6, 1, n_ride)
    qT_ref, kprev_ref, kcur_ref, vprev_ref, vcur_ref, sink_ref = ins
    sa_ref, sb_ref, mxa_ref, mxb_ref = scratch
    i = pl.program_id(0)
    width = group * blk
    kj = lax.broadcasted_iota(jnp.int32, (blk, width), 0)
    qi = lax.broadcasted_iota(jnp.int32, (blk, width), 1) & (blk - 1)
    in_cur = kj <= qi
    cur01 = jnp.where(in_cur, 1.0, 0.0).astype(BF16)
    no_prev_bias = jnp.where(i > 0, 0.0, MASKED)
    kv_dim = n_kv_heads * head_dim

    def score(c, g, s_ref, mx_ref):
        lo, hi = c * blk, (c + 1) * blk
        if c == 0:
            k_band = jnp.concatenate([kprev_ref[...], kcur_ref[lo:hi, :]], axis=0)
        else:
            k_band = kcur_ref[lo - blk:hi, :]
        heads = range(g * group, (g + 1) * group)
        q_g = jnp.concatenate(
            [qT_ref[h * head_dim:(h + 1) * head_dim, lo:hi] for h in heads], axis=1)
        pads = [jnp.zeros((g * head_dim, width), BF16)] if g else []
        pads.append(q_g)
        if g + 1 < n_kv_heads:
            pads.append(jnp.zeros((kv_dim - (g + 1) * head_dim, width), BF16))
        q_pad = jnp.concatenate(pads, axis=0) if len(pads) > 1 else q_g
        s_band = jnp.dot(k_band, q_pad, preferred_element_type=F32)
        s_prev = s_band[:blk] + no_prev_bias if c == 0 else s_band[:blk]
        s = jnp.where(in_cur, s_band[blk:], s_prev)
        s_ref[...] = s
        mx_ref[...] = jnp.max(s, axis=0, keepdims=True)

    def attend(c, g, s_ref, mx_ref):
        lo, hi = c * blk, (c + 1) * blk
        sink = sink_ref[:, g * width:(g + 1) * width] * LOG2_E
        m = jnp.maximum(mx_ref[...], sink)
        p = jnp.exp2(s_ref[...] - m)
        denom = jnp.sum(p, axis=0, keepdims=True) + jnp.exp2(sink - m)
        p = p.astype(BF16)
        p_cur = p * cur01
        p_band = jnp.concatenate([p - p_cur, p_cur], axis=0)
        rows = slice(g * head_dim, (g + 1) * head_dim)
        if c == 0:
            v_band = jnp.concatenate([vprev_ref[rows, :], vcur_ref[rows, lo:hi]], axis=1)
        else:
            v_band = vcur_ref[rows, lo - blk:hi]
        oT = jnp.dot(v_band, p_band, preferred_element_type=F32)
        oT = oT / denom
        o_hd = jnp.concatenate([oT[:, h * blk:(h + 1) * blk] for h in range(group)], axis=0)
        o_ref[lo:hi, g * group * head_dim:(g + 1) * group * head_dim] = o_hd.T.astype(BF16)

    todo = [(c, g) for c in range(n_sub) for g in range(n_kv_heads)]
    bufs = [(sa_ref, mxa_ref), (sb_ref, mxb_ref)]
    score(*todo[0], *bufs[0])
    for n, (c, g) in enumerate(todo):
        if n + 1 < len(todo):
            score(*todo[n + 1], *bufs[(n + 1) % 2])
        attend(c, g, *bufs[n % 2])


def _swa_attention(qT, kn, vT, sink_row, *, n_kv_heads, head_dim, blk, ride=(), tq=512):
    nq, seq = qT.shape
    kv_dim = n_kv_heads * head_dim
    group = nq // kv_dim
    n_sub = tq // blk
    prev_blk = lambda i: jnp.maximum(i * n_sub - 1, 0)
    grid = (seq // tq,)
    rider = _RideAlong(ride, grid, lambda i: i)
    return pl.pallas_call(
        functools.partial(_swa_kernel, n_kv_heads=n_kv_heads, group=group,
                          head_dim=head_dim, blk=blk, n_sub=n_sub, n_ride=len(rider.args)),
        grid=grid,
        in_specs=[pl.BlockSpec((nq, tq), lambda i: (0, i)),
                  pl.BlockSpec((blk, kv_dim), lambda i: (prev_blk(i), 0)),
                  pl.BlockSpec((tq, kv_dim), lambda i: (i, 0)),
                  pl.BlockSpec((kv_dim, blk), lambda i: (0, prev_blk(i))),
                  pl.BlockSpec((kv_dim, tq), lambda i: (0, i)),
                  _resident(sink_row.shape)] + rider.in_specs,
        out_specs=[pl.BlockSpec((tq, nq), lambda i: (i, 0))] + rider.out_specs,
        out_shape=[jax.ShapeDtypeStruct((seq, nq), BF16)] + rider.out_shape,
        scratch_shapes=[pltpu.VMEM((blk, group * blk), F32),
                        pltpu.VMEM((blk, group * blk), F32),
                        pltpu.VMEM((1, group * blk), F32),
                        pltpu.VMEM((1, group * blk), F32)],
        compiler_params=_params(("parallel",)),
        name="swa_attention",
    )(qT, kn, kn, vT, vT, sink_row, *rider.args)


def _moba_kernel(*refs, n_kv_heads, group, head_dim, blk, topk, n_ride):
    (qT_ref, kn_ref, vT_ref, kbar_ref), (o_ref,), scratch = _RideAlong.split(refs, 4, 1, n_ride)
    for g in range(n_kv_heads):
        q_rows = slice(g * group * head_dim, (g + 1) * group * head_dim)
        kv_cols = slice(g * head_dim, (g + 1) * head_dim)
        _moba_tile(qT_ref.at[q_rows, :], kn_ref.at[:, kv_cols], vT_ref.at[kv_cols, :],
                   kbar_ref.at[:, kv_cols], o_ref.at[:, q_rows], *scratch,
                   group=group, head_dim=head_dim, blk=blk, topk=topk)


def _moba_tile(qT_ref, kn_ref, vT_ref, kbar_ref, o_ref,
               q_all_ref, bias_ref, sa_ref, sb_ref, mxa_ref, mxb_ref, m_ref, acc_ref,
               *, group, head_dim, blk, topk):
    qt = pl.program_id(0)
    width = group * blk
    nb = kbar_ref.shape[0]
    own_row = bias_ref.shape[0] - nb - 1
    ones_rows = jnp.ones((acc_ref.shape[0] - head_dim, blk), BF16)
    q_all_ref[...] = jnp.concatenate(
        [qT_ref[h * head_dim:(h + 1) * head_dim, :] for h in range(group)], axis=1)
    q_all = q_all_ref[...]

    gate = jnp.dot(kbar_ref[...].astype(BF16), q_all, preferred_element_type=F32)
    row = lax.broadcasted_iota(jnp.int32, (nb, width), 0).astype(F32)
    past = row < qt.astype(F32)
    gate = jnp.where(past, gate, -jnp.inf)
    chosen = jnp.zeros((nb, width), F32)
    for _ in range(topk):
        best = jnp.max(gate, axis=0, keepdims=True)
        idx = jnp.min(jnp.where(gate == best, row, float(nb)), axis=0, keepdims=True)
        hit = row == idx
        chosen = jnp.where(hit, 1.0, chosen)
        gate = jnp.where(hit, -jnp.inf, gate)
    bias_ref[own_row:own_row + 1, :] = jnp.zeros((1, width), F32)
    bias_ref[own_row + 1:, :] = jnp.where(jnp.logical_and(chosen > 0.0, past), 0.0, MASKED)

    def scores(block):
        start = pl.multiple_of(block * blk, blk)
        return jnp.dot(kn_ref[pl.ds(start, blk), :], q_all_ref[...], preferred_element_type=F32)

    def stash(s, s_ref, mx_ref):
        s_ref[...] = s
        mx_ref[...] = jnp.max(s, axis=0, keepdims=True)

    def absorb(t, block, s_ref, mx_ref):
        bias = bias_ref[pl.ds(own_row + t, 1), :]
        m_old = m_ref[...]
        m_new = jnp.maximum(m_old, mx_ref[...] + bias)
        alpha = jnp.exp2(m_old - m_new)
        p = jnp.exp2(s_ref[...] - (m_new - bias))
        start = pl.multiple_of(block * blk, blk)
        v_ones = jnp.concatenate([vT_ref[:, pl.ds(start, blk)], ones_rows], axis=0)
        acc_ref[...] = alpha * acc_ref[...] + jnp.dot(
            v_ones, p.astype(BF16), preferred_element_type=F32)
        m_ref[...] = m_new

    m_ref[...] = jnp.full((1, width), MASKED, F32)
    acc_ref[...] = jnp.zeros(acc_ref.shape, F32)
    kj = lax.broadcasted_iota(jnp.int32, (blk, width), 0)
    qi = lax.broadcasted_iota(jnp.int32, (blk, width), 1) & (blk - 1)
    stash(jnp.where(kj <= qi, scores(qt), MASKED), sa_ref, mxa_ref)

    def block_of(t):
        return jnp.where(t == 0, qt, t - 1)

    def absorb_pairs(t, count):
        for k in range(0, count, 2):
            stash(scores(t + k), sb_ref, mxb_ref)
            absorb(t + k, block_of(t + k), sa_ref, mxa_ref)
            stash(scores(t + k + 1), sa_ref, mxa_ref)
            absorb(t + k + 1, t + k, sb_ref, mxb_ref)

    def octet(i, carry):
        absorb_pairs(8 * i, 8)
        return carry

    n_octets = lax.shift_right_logical(qt, 3)
    rest = qt & 7
    lax.fori_loop(0, n_octets, octet, 0)

    @pl.when(rest >= 4)
    def _():
        absorb_pairs(8 * n_octets, 4)

    t_rest = 8 * n_octets + (rest & 4)
    rest = rest & 3

    @pl.when(rest >= 2)
    def _():
        absorb_pairs(t_rest, 2)

    @pl.when((rest & 1) == 1)
    def _():
        t = t_rest + (rest & 2)
        stash(scores(t), sb_ref, mxb_ref)
        absorb(t, block_of(t), sa_ref, mxa_ref)
        absorb(t + 1, t, sb_ref, mxb_ref)

    @pl.when((rest & 1) == 0)
    def _():
        absorb(qt, block_of(qt), sa_ref, mxa_ref)

    oT = acc_ref[:head_dim, :] / acc_ref[head_dim:head_dim + 1, :]
    o_hd = jnp.concatenate([oT[:, h * blk:(h + 1) * blk] for h in range(group)], axis=0)
    o_ref[...] = o_hd.T.astype(BF16)


def _moba_attention(qT, kn, vT, kbar, *, n_kv_heads, head_dim, blk, topk, ride=()):
    nq, seq = qT.shape
    group = nq // (n_kv_heads * head_dim)
    width = group * blk
    nb = seq // blk
    kv_dim = n_kv_heads * head_dim
    grid = (nb,)
    rider = _RideAlong(ride, grid, lambda t: t)
    return pl.pallas_call(
        functools.partial(_moba_kernel, n_kv_heads=n_kv_heads, group=group, head_dim=head_dim,
                          blk=blk, topk=topk, n_ride=len(rider.args)),
        grid=grid,
        in_specs=[pl.BlockSpec((nq, blk), lambda t: (0, t)),
                  _resident((seq, kv_dim)),
                  _resident((kv_dim, seq)),
                  _resident((nb, kv_dim))] + rider.in_specs,
        out_specs=[pl.BlockSpec((blk, nq), lambda t: (t, 0))] + rider.out_specs,
        out_shape=[jax.ShapeDtypeStruct((seq, nq), BF16)] + rider.out_shape,
        scratch_shapes=[pltpu.VMEM((head_dim, width), BF16),
                        pltpu.VMEM((nb + SUBLANES, width), F32),
                        pltpu.VMEM((blk, width), F32),
                        pltpu.VMEM((blk, width), F32),
                        pltpu.VMEM((1, width), F32),
                        pltpu.VMEM((1, width), F32),
                        pltpu.VMEM((1, width), F32),
                        pltpu.VMEM((head_dim + BF16_SUBLANES, width), F32)],
        compiler_params=_params(("parallel",)),
        name="moba_attention",
    )(qT, kn, vT, kbar, *rider.args)


def _oproj_ln_kernel(o_ref, w_ref, x_ref, g_ref, b_ref, h_ref, *, row_chunks=4):
    rows = o_ref.shape[0] // row_chunks
    for r in range(row_chunks):
        sl = slice(r * rows, (r + 1) * rows)
        y = jnp.dot(o_ref[sl, :], w_ref[...], preferred_element_type=F32)
        h_ref[sl, :] = _layer_norm(DEEPNORM_ALPHA * x_ref[sl, :] + y, g_ref[...], b_ref[...])


def _oproj_ln(o, w, x, g_row, b_row, tm=512):
    seq, d_model = x.shape
    k_dim = o.shape[1]
    return pl.pallas_call(
        _oproj_ln_kernel,
        grid=(seq // tm,),
        in_specs=[pl.BlockSpec((tm, k_dim), lambda i: (i, 0)),
                  _resident((k_dim, d_model)),
                  pl.BlockSpec((tm, d_model), lambda i: (i, 0)),
                  _resident((1, d_model)),
                  _resident((1, d_model))],
        out_specs=pl.BlockSpec((tm, d_model), lambda i: (i, 0)),
        out_shape=jax.ShapeDtypeStruct((seq, d_model), F32),
        compiler_params=_params(("parallel",)),
        name="oproj_ln",
    )(o, w, x, g_row, b_row)


def _mlp_ln_kernel(*refs, n_ride, row_chunks=2):
    ins, (out_ref,), (hb_ref, acc_ref) = _RideAlong.split(refs, 5, 1, n_ride)
    h_ref, win_ref, wout_ref, g_ref, b_ref = ins
    f = pl.program_id(1)
    last = pl.num_programs(1) - 1

    def hidden(rows):
        a = jnp.dot(hb_ref[rows, :], win_ref[...], preferred_element_type=F32)
        a = jnp.maximum(a, 0.0)
        a = a * a
        return jnp.dot(a.astype(BF16), wout_ref[...], preferred_element_type=F32)

    @pl.when(f == 0)
    def _():
        hb_ref[...] = h_ref[...].astype(BF16)
        acc_ref[...] = hidden(slice(None))

    @pl.when(jnp.logical_and(f > 0, f < last))
    def _():
        acc_ref[...] += hidden(slice(None))

    @pl.when(f == last)
    def _():
        rows = hb_ref.shape[0] // row_chunks
        for r in range(row_chunks):
            sl = slice(r * rows, (r + 1) * rows)
            y = acc_ref[sl, :] + hidden(sl)
            out_ref[sl, :] = _layer_norm(DEEPNORM_ALPHA * h_ref[sl, :] + y,
                                         g_ref[...], b_ref[...])


def _mlp_ln(h, w_in, w_out, g_row, b_row, ride=(), tm=512, tf=1024):
    seq, d_model = h.shape
    d_ff = w_in.shape[1]
    grid = (seq // tm, d_ff // tf)
    assert grid[1] >= 2
    step_of = lambda i, f: i * grid[1] + f
    rider = _RideAlong(ride, grid, step_of)
    return pl.pallas_call(
        functools.partial(_mlp_ln_kernel, n_ride=len(rider.args)),
        grid=grid,
        in_specs=[pl.BlockSpec((tm, d_model), lambda i, f: (i, 0)),
                  pl.BlockSpec((d_model, tf), lambda i, f: (0, f)),
                  pl.BlockSpec((tf, d_model), lambda i, f: (f, 0)),
                  _resident((1, d_model)),
                  _resident((1, d_model))] + rider.in_specs,
        out_specs=[pl.BlockSpec((tm, d_model), lambda i, f: (i, 0))] + rider.out_specs,
        out_shape=[jax.ShapeDtypeStruct((seq, d_model), F32)] + rider.out_shape,
        scratch_shapes=[pltpu.VMEM((tm, d_model), BF16), pltpu.VMEM((tm, d_model), F32)],
        compiler_params=_params(("parallel", "arbitrary")),
        name="mlp_ln",
    )(h, w_in, w_out, g_row, b_row, *rider.args)


def _round_kernel(w_ref, o_ref):
    o_ref[...] = w_ref[...].astype(BF16)


def _round_layer(w_all, layer, slab=256):
    _, rows, cols = w_all.shape
    return pl.pallas_call(
        _round_kernel,
        grid=(rows // slab,),
        in_specs=[pl.BlockSpec((None, slab, cols), lambda i: (layer, i, 0))],
        out_specs=pl.BlockSpec((slab, cols), lambda i: (i, 0)),
        out_shape=jax.ShapeDtypeStruct((rows, cols), BF16),
        compiler_params=_params(("parallel",)),
        name="round_layer",
    )(w_all)


def kernel(x, positions, a_w_qkv, a_b_qkv, a_sinks, a_w_o, b_w_q, b_w_o, b_w_kv,
           ln_attn_g, ln_attn_b, ln_mlp_g, ln_mlp_b, mlp_w_in, mlp_w_out):
    batch, seq, d_model = x.shape
    assert batch == 1 and a_w_qkv.shape[0] == 1 and b_w_q.shape[0] == 1
    h = x.reshape(seq, d_model)
    pos_row = positions.reshape(1, seq)
    row = lambda v: v.reshape(1, -1)


    n_a_heads = a_sinks.shape[1]
    qT, kn, vT, w_in0 = _project(
        h, [_round_layer(a_w_qkv, 0)], a_b_qkv, pos_row,
        n_q_heads=n_a_heads, n_kv_heads=A_KV_HEADS, head_dim=A_HEAD_DIM,
        q_scale=A_HEAD_DIM ** -0.5 * LOG2_E, ride=[(mlp_w_in, 0)])
    sink_row = jnp.repeat(a_sinks[0].astype(F32), A_WINDOW)[None, :]
    o, w_out0, w_o0 = _swa_attention(
        qT, kn, vT, sink_row, n_kv_heads=A_KV_HEADS, head_dim=A_HEAD_DIM, blk=A_WINDOW,
        ride=[(mlp_w_out, 0), (a_w_o, 0)])
    h = _oproj_ln(o, w_o0, h, row(ln_attn_g[0]), row(ln_attn_b[0]))
    h, w_q1, w_kv1 = _mlp_ln(h, w_in0, w_out0, row(ln_mlp_g[0]), row(ln_mlp_b[0]),
                             ride=[(b_w_q, 0), (b_w_kv[None], 0)])

    n_b_heads = b_w_q.shape[2] // B_HEAD_DIM
    qT, kn, vT, kbar, w_o1 = _project(
        h, [w_q1, w_kv1], None,
        pos_row, n_q_heads=n_b_heads, n_kv_heads=B_KV_HEADS, head_dim=B_HEAD_DIM,
        q_scale=B_HEAD_DIM ** -0.5 * LOG2_E, kbar_block=MOBA_BLOCK, ride=[(b_w_o, 0)])
    kbar = kbar.reshape(seq // MOBA_BLOCK, B_KV_HEADS * B_HEAD_DIM)
    o, w_in1, w_out1 = _moba_attention(
        qT, kn, vT, kbar, n_kv_heads=B_KV_HEADS, head_dim=B_HEAD_DIM, blk=MOBA_BLOCK,
        topk=MOBA_TOPK, ride=[(mlp_w_in, 1), (mlp_w_out, 1)])
    h = _oproj_ln(o, w_o1, h, row(ln_attn_g[1]), row(ln_attn_b[1]))
    (h,) = _mlp_ln(h, w_in1, w_out1, row(ln_mlp_g[1]), row(ln_mlp_b[1]))
    return h.reshape(batch, seq, d_model)
```

```python
import functools
import math

import jax
import jax.numpy as jnp
from jax import lax
from jax.experimental import pallas as pl
from jax.experimental.pallas import tpu as pltpu

A_HEAD_DIM = 64
A_KV_HEADS = 4
A_WINDOW = 128
B_HEAD_DIM = 128
B_KV_HEADS = 4
MOBA_BLOCK = 256
MOBA_TOPK = 3
ROPE_THETA = 10000.0
LN_EPS = 1e-5
DEPTH = 2
DEEPNORM_ALPHA = (2 * DEPTH) ** 0.25
LOG2_E = math.log2(math.e)

V7X_VMEM_BYTES = 64 * 1024 * 1024
V7X_SCOPED_VMEM_BYTES = 60 * 1024 * 1024
SUBLANES = 8
BF16_SUBLANES = 16

BF16 = jnp.bfloat16
F32 = jnp.float32
MASKED = -1e30


def _resident(shape):
    return pl.BlockSpec(shape, lambda *_: (0,) * len(shape), pipeline_mode=pl.Buffered(1))


def _params(semantics, vmem_bytes, claim_all=False):
    limit = V7X_SCOPED_VMEM_BYTES if claim_all else min(int(vmem_bytes), V7X_VMEM_BYTES)
    return pltpu.CompilerParams(dimension_semantics=semantics, vmem_limit_bytes=limit)


class _RideAlong:
    def __init__(self, weights, grid, step_of):
        steps = 1
        for n in grid:
            steps *= n
        self.args = [w for w, _ in weights]
        self.in_specs, self.out_specs, self.out_shape, self.vmem = [], [], [], 0
        for w, layer in weights:
            _, rows, cols = w.shape
            slab = rows // steps
            assert slab * steps == rows and slab % BF16_SUBLANES == 0
            self.in_specs.append(pl.BlockSpec(
                (None, slab, cols), lambda *g, layer=layer: (layer, step_of(*g), 0)))
            self.out_specs.append(pl.BlockSpec((slab, cols), lambda *g: (step_of(*g), 0)))
            self.out_shape.append(jax.ShapeDtypeStruct((rows, cols), BF16))
            self.vmem += 2 * slab * cols * (4 + 2)

    @staticmethod
    def split(refs, n_in, n_out, n_ride):
        ins, rest = refs[:n_in], refs[n_in:]
        ride_in, rest = rest[:n_ride], rest[n_ride:]
        outs, rest = rest[:n_out], rest[n_out:]
        ride_out, scratch = rest[:n_ride], rest[n_ride:]
        for src, dst in zip(ride_in, ride_out):
            dst[...] = src[...].astype(BF16)
        return ins, outs, scratch


def _layer_norm(z, g, b):
    mu = jnp.mean(z, axis=-1, keepdims=True)
    zc = z - mu
    var = jnp.mean(zc * zc, axis=-1, keepdims=True)
    return zc * lax.rsqrt(var + LN_EPS) * g + b


def _rope_table_kernel(pos_ref, inv_ref, cos_ref, sin_ref):
    ang = pos_ref[...].astype(F32) * inv_ref[...]
    cos_ref[...] = jnp.cos(ang)
    sin_ref[...] = jnp.sin(ang)


def _rope_tables(pos_row, head_dim, ts=2048):
    seq = pos_row.shape[1]
    half = head_dim // 2
    inv_freq = ROPE_THETA ** (-(jnp.arange(half, dtype=F32) * 2.0) / head_dim)
    out = jax.ShapeDtypeStruct((half, seq), F32)
    return pl.pallas_call(
        _rope_table_kernel,
        grid=(seq // ts,),
        in_specs=[pl.BlockSpec((1, ts), lambda i: (0, i)),
                  pl.BlockSpec((half, 1), lambda i: (0, 0))],
        out_specs=[pl.BlockSpec((half, ts), lambda i: (0, i))] * 2,
        out_shape=[out, out],
        name="rope_tables",
    )(pos_row, inv_freq[:, None])


def _rope_rows(y, base, half, c, s):
    x1 = y[base:base + half]
    x2 = y[base + half:base + 2 * half]
    return x1 * c - x2 * s, x2 * c + x1 * s


def _proj_kernel(*refs, w_cols, has_bias, n_q_heads, n_kv_heads, head_dim, q_scale,
                 kbar_block, n_ride, chunk=512):
    n_in = 3 + has_bias + len(w_cols)
    ins, outs, _ = _RideAlong.split(refs, n_in, 4 if kbar_block else 3, n_ride)
    x_ref, cos_ref, sin_ref = ins[:3]
    b_ref = ins[3] if has_bias else None
    w_refs = ins[3 + has_bias:]
    qT_ref, kn_ref, vT_ref = outs[:3]
    half = head_dim // 2
    nq = n_q_heads * head_dim
    nk = n_kv_heads * head_dim
    xb = x_ref[...].astype(BF16)
    c = cos_ref[...]
    s = sin_ref[...]

    def project_rows(lo, hi):
        base = 0
        for w_ref, cols in zip(w_refs, w_cols):
            if lo < base + cols:
                assert hi <= base + cols
                y = jnp.dot(xb, w_ref[:, lo - base:hi - base], preferred_element_type=F32)
                if has_bias:
                    y = y + b_ref[:, lo:hi]
                return y.T
            base += cols

    for base in range(0, nq, chunk):
        y = project_rows(base, base + chunk)
        for h in range(chunk // head_dim):
            r1, r2 = _rope_rows(y, h * head_dim, half, c, s)
            lo = base + h * head_dim
            qT_ref[lo:lo + half, :] = (r1 * q_scale).astype(BF16)
            qT_ref[lo + half:lo + head_dim, :] = (r2 * q_scale).astype(BF16)
    y = project_rows(nq, nq + nk)
    k_rows = []
    for g in range(n_kv_heads):
        k_rows.extend(_rope_rows(y, g * head_dim, half, c, s))
    k_nat = jnp.concatenate(k_rows, axis=0).T
    kn_ref[...] = k_nat.astype(BF16)
    vT_ref[...] = project_rows(nq + nk, nq + 2 * nk).astype(BF16)
    if kbar_block:
        kbar_ref = outs[3]
        tm = k_nat.shape[0]
        for blk in range(tm // kbar_block):
            kbar_ref[0, blk:blk + 1, :] = jnp.mean(
                k_nat[blk * kbar_block:(blk + 1) * kbar_block], axis=0, keepdims=True)


def _project(x, ws, b_row, cos, sin, *, n_q_heads, n_kv_heads, head_dim, q_scale,
             kbar_block=0, ride=(), tm=512):
    seq, d_model = x.shape
    w_cols = tuple(w.shape[1] for w in ws)
    n_out = sum(w_cols)
    bias = [] if b_row is None else [b_row]
    nq = n_q_heads * head_dim
    nk = n_kv_heads * head_dim
    half = head_dim // 2
    out_shape = [jax.ShapeDtypeStruct((nq, seq), BF16),
                 jax.ShapeDtypeStruct((seq, nk), BF16),
                 jax.ShapeDtypeStruct((nk, seq), BF16)]
    out_specs = [pl.BlockSpec((nq, tm), lambda i: (0, i)),
                 pl.BlockSpec((tm, nk), lambda i: (i, 0)),
                 pl.BlockSpec((nk, tm), lambda i: (0, i))]
    if kbar_block:
        nb = tm // kbar_block
        out_shape.append(jax.ShapeDtypeStruct((seq // tm, nb, nk), F32))
        out_specs.append(pl.BlockSpec((1, nb, nk), lambda i: (i, 0, 0)))
    grid = (seq // tm,)
    rider = _RideAlong(ride, grid, lambda i: i)
    vmem = (2 * tm * d_model * 4 + n_out * d_model * 2 + n_out * SUBLANES * 4
            + 2 * n_out * tm * 2 + 4 * n_out * tm * 4 + rider.vmem)
    return pl.pallas_call(
        functools.partial(_proj_kernel, w_cols=w_cols, has_bias=b_row is not None,
                          n_q_heads=n_q_heads, n_kv_heads=n_kv_heads, head_dim=head_dim,
                          q_scale=q_scale, kbar_block=kbar_block, n_ride=len(rider.args)),
        grid=grid,
        in_specs=[pl.BlockSpec((tm, d_model), lambda i: (i, 0)),
                  pl.BlockSpec((half, tm), lambda i: (0, i)),
                  pl.BlockSpec((half, tm), lambda i: (0, i))]
                 + [_resident(b.shape) for b in bias]
                 + [_resident(w.shape) for w in ws] + rider.in_specs,
        out_specs=out_specs + rider.out_specs,
        out_shape=out_shape + rider.out_shape,
        compiler_params=_params(("parallel",), vmem),
        name="qkv_proj_hd%d" % head_dim,
    )(x, cos, sin, *bias, *ws, *rider.args)


def _swa_kernel(*refs, n_kv_heads, group, head_dim, blk, n_sub, n_ride):
    ins, (o_ref,), scratch = _RideAlong.split(refs, 6, 1, n_ride)
    qT_ref, kprev_ref, kcur_ref, vprev_ref, vcur_ref, sink_ref = ins
    sa_ref, sb_ref, mxa_ref, mxb_ref = scratch
    i = pl.program_id(0)
    width = group * blk
    kj = lax.broadcasted_iota(jnp.int32, (blk, width), 0)
    qi = lax.broadcasted_iota(jnp.int32, (blk, width), 1) & (blk - 1)
    in_cur = kj <= qi
    cur01 = jnp.where(in_cur, 1.0, 0.0).astype(BF16)
    no_prev_bias = jnp.where(i > 0, 0.0, MASKED)
    kv_dim = n_kv_heads * head_dim

    def score(c, g, s_ref, mx_ref):
        lo, hi = c * blk, (c + 1) * blk
        if c == 0:
            k_band = jnp.concatenate([kprev_ref[...], kcur_ref[lo:hi, :]], axis=0)
        else:
            k_band = kcur_ref[lo - blk:hi, :]
        heads = range(g * group, (g + 1) * group)
        q_g = jnp.concatenate(
            [qT_ref[h * head_dim:(h + 1) * head_dim, lo:hi] for h in heads], axis=1)
        pads = [jnp.zeros((g * head_dim, width), BF16)] if g else []
        pads.append(q_g)
        if g + 1 < n_kv_heads:
            pads.append(jnp.zeros((kv_dim - (g + 1) * head_dim, width), BF16))
        q_pad = jnp.concatenate(pads, axis=0) if len(pads) > 1 else q_g
        s_band = jnp.dot(k_band, q_pad, preferred_element_type=F32)
        s_prev = s_band[:blk] + no_prev_bias if c == 0 else s_band[:blk]
        s = jnp.where(in_cur, s_band[blk:], s_prev)
        s_ref[...] = s
        mx_ref[...] = jnp.max(s, axis=0, keepdims=True)

    def attend(c, g, s_ref, mx_ref):
        lo, hi = c * blk, (c + 1) * blk
        sink = sink_ref[:, g * width:(g + 1) * width] * LOG2_E
        m = jnp.maximum(mx_ref[...], sink)
        p = jnp.exp2(s_ref[...] - m).astype(BF16)
        p_cur = p * cur01
        p_band = jnp.concatenate([p - p_cur, p_cur], axis=0)
        rows = slice(g * head_dim, (g + 1) * head_dim)
        if c == 0:
            v_band = jnp.concatenate([vprev_ref[rows, :], vcur_ref[rows, lo:hi]], axis=1)
        else:
            v_band = vcur_ref[rows, lo - blk:hi]
        v_ones = jnp.concatenate([v_band, jnp.ones((BF16_SUBLANES, 2 * blk), BF16)], axis=0)
        oT = jnp.dot(v_ones, p_band, preferred_element_type=F32)
        denom = oT[head_dim:head_dim + 1] + jnp.exp2(sink - m)
        oT = oT[:head_dim] / denom
        o_hd = jnp.concatenate([oT[:, h * blk:(h + 1) * blk] for h in range(group)], axis=0)
        o_ref[lo:hi, g * group * head_dim:(g + 1) * group * head_dim] = o_hd.T.astype(BF16)

    todo = [(c, g) for c in range(n_sub) for g in range(n_kv_heads)]
    bufs = [(sa_ref, mxa_ref), (sb_ref, mxb_ref)]
    score(*todo[0], *bufs[0])
    for n, (c, g) in enumerate(todo):
        if n + 1 < len(todo):
            score(*todo[n + 1], *bufs[(n + 1) % 2])
        attend(c, g, *bufs[n % 2])


def _swa_attention(qT, kn, vT, sink_row, *, n_kv_heads, head_dim, blk, ride=(), tq=512):
    nq, seq = qT.shape
    kv_dim = n_kv_heads * head_dim
    group = nq // kv_dim
    n_sub = tq // blk
    prev_blk = lambda i: jnp.maximum(i * n_sub - 1, 0)
    grid = (seq // tq,)
    rider = _RideAlong(ride, grid, lambda i: i)
    vmem = (2 * (nq * tq * 2 * 2 + 2 * (tq + blk) * kv_dim * 2) + 16 * blk * group * blk * 4
            + rider.vmem)
    return pl.pallas_call(
        functools.partial(_swa_kernel, n_kv_heads=n_kv_heads, group=group,
                          head_dim=head_dim, blk=blk, n_sub=n_sub, n_ride=len(rider.args)),
        grid=grid,
        in_specs=[pl.BlockSpec((nq, tq), lambda i: (0, i)),
                  pl.BlockSpec((blk, kv_dim), lambda i: (prev_blk(i), 0)),
                  pl.BlockSpec((tq, kv_dim), lambda i: (i, 0)),
                  pl.BlockSpec((kv_dim, blk), lambda i: (0, prev_blk(i))),
                  pl.BlockSpec((kv_dim, tq), lambda i: (0, i)),
                  _resident(sink_row.shape)] + rider.in_specs,
        out_specs=[pl.BlockSpec((tq, nq), lambda i: (i, 0))] + rider.out_specs,
        out_shape=[jax.ShapeDtypeStruct((seq, nq), BF16)] + rider.out_shape,
        scratch_shapes=[pltpu.VMEM((blk, group * blk), F32),
                        pltpu.VMEM((blk, group * blk), F32),
                        pltpu.VMEM((1, group * blk), F32),
                        pltpu.VMEM((1, group * blk), F32)],
        compiler_params=_params(("parallel",), vmem, claim_all=True),
        name="swa_attention",
    )(qT, kn, kn, vT, vT, sink_row, *rider.args)


def _moba_kernel(*refs, n_kv_heads, group, head_dim, blk, topk, n_ride):
    (qT_ref, kn_ref, vT_ref, kbar_ref), (o_ref,), scratch = _RideAlong.split(refs, 4, 1, n_ride)
    for g in range(n_kv_heads):
        q_rows = slice(g * group * head_dim, (g + 1) * group * head_dim)
        kv_cols = slice(g * head_dim, (g + 1) * head_dim)
        _moba_tile(qT_ref.at[q_rows, :], kn_ref.at[:, kv_cols], vT_ref.at[kv_cols, :],
                   kbar_ref.at[:, kv_cols], o_ref.at[:, q_rows], *scratch,
                   group=group, head_dim=head_dim, blk=blk, topk=topk)


def _moba_tile(qT_ref, kn_ref, vT_ref, kbar_ref, o_ref,
               q_all_ref, bias_ref, sa_ref, sb_ref, mxa_ref, mxb_ref, m_ref, acc_ref,
               *, group, head_dim, blk, topk):
    qt = pl.program_id(0)
    width = group * blk
    nb = kbar_ref.shape[0]
    own_row = bias_ref.shape[0] - nb - 1
    ones_rows = jnp.ones((acc_ref.shape[0] - head_dim, blk), BF16)
    q_all_ref[...] = jnp.concatenate(
        [qT_ref[h * head_dim:(h + 1) * head_dim, :] for h in range(group)], axis=1)
    q_all = q_all_ref[...]

    gate = jnp.dot(kbar_ref[...].astype(BF16), q_all, preferred_element_type=F32)
    row = lax.broadcasted_iota(jnp.int32, (nb, width), 0).astype(F32)
    past = row < qt.astype(F32)
    gate = jnp.where(past, gate, -jnp.inf)
    chosen = jnp.zeros((nb, width), F32)
    for _ in range(topk):
        best = jnp.max(gate, axis=0, keepdims=True)
        idx = jnp.min(jnp.where(gate == best, row, float(nb)), axis=0, keepdims=True)
        hit = row == idx
        chosen = jnp.where(hit, 1.0, chosen)
        gate = jnp.where(hit, -jnp.inf, gate)
    bias_ref[own_row:own_row + 1, :] = jnp.zeros((1, width), F32)
    bias_ref[own_row + 1:, :] = jnp.where(jnp.logical_and(chosen > 0.0, past), 0.0, MASKED)

    def scores(block):
        start = pl.multiple_of(block * blk, blk)
        return jnp.dot(kn_ref[pl.ds(start, blk), :], q_all_ref[...], preferred_element_type=F32)

    def stash(s, s_ref, mx_ref):
        s_ref[...] = s
        mx_ref[...] = jnp.max(s, axis=0, keepdims=True)

    def absorb(t, block, s_ref, mx_ref):
        bias = bias_ref[pl.ds(own_row + t, 1), :]
        m_old = m_ref[...]
        m_new = jnp.maximum(m_old, mx_ref[...] + bias)
        alpha = jnp.exp2(m_old - m_new)
        p = jnp.exp2(s_ref[...] - (m_new - bias))
        start = pl.multiple_of(block * blk, blk)
        v_ones = jnp.concatenate([vT_ref[:, pl.ds(start, blk)], ones_rows], axis=0)
        acc_ref[...] = alpha * acc_ref[...] + jnp.dot(
            v_ones, p.astype(BF16), preferred_element_type=F32)
        m_ref[...] = m_new

    m_ref[...] = jnp.full((1, width), MASKED, F32)
    acc_ref[...] = jnp.zeros(acc_ref.shape, F32)
    kj = lax.broadcasted_iota(jnp.int32, (blk, width), 0)
    qi = lax.broadcasted_iota(jnp.int32, (blk, width), 1) & (blk - 1)
    stash(jnp.where(kj <= qi, scores(qt), MASKED), sa_ref, mxa_ref)

    def block_of(t):
        return jnp.where(t == 0, qt, t - 1)

    def absorb_pairs(t, count):
        for k in range(0, count, 2):
            stash(scores(t + k), sb_ref, mxb_ref)
            absorb(t + k, block_of(t + k), sa_ref, mxa_ref)
            stash(scores(t + k + 1), sa_ref, mxa_ref)
            absorb(t + k + 1, t + k, sb_ref, mxb_ref)

    def octet(i, carry):
        absorb_pairs(8 * i, 8)
        return carry

    n_octets = lax.shift_right_logical(qt, 3)
    rest = qt & 7
    lax.fori_loop(0, n_octets, octet, 0)

    @pl.when(rest >= 4)
    def _():
        absorb_pairs(8 * n_octets, 4)

    t_rest = 8 * n_octets + (rest & 4)
    rest = rest & 3

    @pl.when(rest >= 2)
    def _():
        absorb_pairs(t_rest, 2)

    @pl.when((rest & 1) == 1)
    def _():
        t = t_rest + (rest & 2)
        stash(scores(t), sb_ref, mxb_ref)
        absorb(t, block_of(t), sa_ref, mxa_ref)
        absorb(t + 1, t, sb_ref, mxb_ref)

    @pl.when((rest & 1) == 0)
    def _():
        absorb(qt, block_of(qt), sa_ref, mxa_ref)

    oT = acc_ref[:head_dim, :] / acc_ref[head_dim:head_dim + 1, :]
    o_hd = jnp.concatenate([oT[:, h * blk:(h + 1) * blk] for h in range(group)], axis=0)
    o_ref[...] = o_hd.T.astype(BF16)


def _moba_attention(qT, kn, vT, kbar, *, n_kv_heads, head_dim, blk, topk, ride=()):
    nq, seq = qT.shape
    group = nq // (n_kv_heads * head_dim)
    width = group * blk
    nb = seq // blk
    kv_dim = n_kv_heads * head_dim
    grid = (nb,)
    rider = _RideAlong(ride, grid, lambda t: t)
    vmem = (2 * seq * kv_dim * 2 + 4 * nq * blk * 2
            + head_dim * width * (2 + 4) + 8 * blk * width * 4 + rider.vmem)
    return pl.pallas_call(
        functools.partial(_moba_kernel, n_kv_heads=n_kv_heads, group=group, head_dim=head_dim,
                          blk=blk, topk=topk, n_ride=len(rider.args)),
        grid=grid,
        in_specs=[pl.BlockSpec((nq, blk), lambda t: (0, t)),
                  _resident((seq, kv_dim)),
                  _resident((kv_dim, seq)),
                  _resident((nb, kv_dim))] + rider.in_specs,
        out_specs=[pl.BlockSpec((blk, nq), lambda t: (t, 0))] + rider.out_specs,
        out_shape=[jax.ShapeDtypeStruct((seq, nq), BF16)] + rider.out_shape,
        scratch_shapes=[pltpu.VMEM((head_dim, width), BF16),
                        pltpu.VMEM((nb + SUBLANES, width), F32),
                        pltpu.VMEM((blk, width), F32),
                        pltpu.VMEM((blk, width), F32),
                        pltpu.VMEM((1, width), F32),
                        pltpu.VMEM((1, width), F32),
                        pltpu.VMEM((1, width), F32),
                        pltpu.VMEM((head_dim + BF16_SUBLANES, width), F32)],
        compiler_params=_params(("parallel",), vmem, claim_all=True),
        name="moba_attention",
    )(qT, kn, vT, kbar, *rider.args)


def _oproj_ln_kernel(o_ref, w_ref, x_ref, g_ref, b_ref, h_ref, *, row_chunks=4):
    rows = o_ref.shape[0] // row_chunks
    for r in range(row_chunks):
        sl = slice(r * rows, (r + 1) * rows)
        y = jnp.dot(o_ref[sl, :], w_ref[...], preferred_element_type=F32)
        h_ref[sl, :] = _layer_norm(DEEPNORM_ALPHA * x_ref[sl, :] + y, g_ref[...], b_ref[...])


def _oproj_ln(o, w, x, g_row, b_row, tm=512):
    seq, d_model = x.shape
    k_dim = o.shape[1]
    vmem = (2 * tm * k_dim * 2 + k_dim * d_model * 2 + 4 * tm * d_model * 4
            + 3 * tm * d_model * 4)
    return pl.pallas_call(
        _oproj_ln_kernel,
        grid=(seq // tm,),
        in_specs=[pl.BlockSpec((tm, k_dim), lambda i: (i, 0)),
                  _resident((k_dim, d_model)),
                  pl.BlockSpec((tm, d_model), lambda i: (i, 0)),
                  _resident((1, d_model)),
                  _resident((1, d_model))],
        out_specs=pl.BlockSpec((tm, d_model), lambda i: (i, 0)),
        out_shape=jax.ShapeDtypeStruct((seq, d_model), F32),
        compiler_params=_params(("parallel",), vmem),
        name="oproj_ln",
    )(o, w, x, g_row, b_row)


def _mlp_ln_kernel(*refs, n_ride, row_chunks=2):
    ins, (out_ref,), (hb_ref, acc_ref) = _RideAlong.split(refs, 5, 1, n_ride)
    h_ref, win_ref, wout_ref, g_ref, b_ref = ins
    f = pl.program_id(1)
    last = pl.num_programs(1) - 1

    @pl.when(f == 0)
    def _():
        hb_ref[...] = h_ref[...].astype(BF16)
        acc_ref[...] = jnp.zeros_like(acc_ref)

    def hidden(rows):
        a = jnp.dot(hb_ref[rows, :], win_ref[...], preferred_element_type=F32)
        a = jnp.maximum(a, 0.0)
        a = a * a
        return jnp.dot(a.astype(BF16), wout_ref[...], preferred_element_type=F32)

    @pl.when(f < last)
    def _():
        acc_ref[...] += hidden(slice(None))

    @pl.when(f == last)
    def _():
        rows = hb_ref.shape[0] // row_chunks
        for r in range(row_chunks):
            sl = slice(r * rows, (r + 1) * rows)
            y = acc_ref[sl, :] + hidden(sl)
            out_ref[sl, :] = _layer_norm(DEEPNORM_ALPHA * h_ref[sl, :] + y,
                                         g_ref[...], b_ref[...])


def _mlp_ln(h, w_in, w_out, g_row, b_row, ride=(), tm=512, tf=1024):
    seq, d_model = h.shape
    d_ff = w_in.shape[1]
    grid = (seq // tm, d_ff // tf)
    step_of = lambda i, f: i * grid[1] + f
    rider = _RideAlong(ride, grid, step_of)
    vmem = (4 * tm * d_model * 4 + 4 * d_model * tf * 2 + tm * d_model * (2 + 4)
            + 2 * tm * tf * 4 + 2 * tm * d_model * 4 + rider.vmem)
    return pl.pallas_call(
        functools.partial(_mlp_ln_kernel, n_ride=len(rider.args)),
        grid=grid,
        in_specs=[pl.BlockSpec((tm, d_model), lambda i, f: (i, 0)),
                  pl.BlockSpec((d_model, tf), lambda i, f: (0, f)),
                  pl.BlockSpec((tf, d_model), lambda i, f: (f, 0)),
                  _resident((1, d_model)),
                  _resident((1, d_model))] + rider.in_specs,
        out_specs=[pl.BlockSpec((tm, d_model), lambda i, f: (i, 0))] + rider.out_specs,
        out_shape=[jax.ShapeDtypeStruct((seq, d_model), F32)] + rider.out_shape,
        scratch_shapes=[pltpu.VMEM((tm, d_model), BF16), pltpu.VMEM((tm, d_model), F32)],
        compiler_params=_params(("parallel", "arbitrary"), vmem),
        name="mlp_ln",
    )(h, w_in, w_out, g_row, b_row, *rider.args)


def _round_kernel(w_ref, o_ref):
    o_ref[...] = w_ref[...].astype(BF16)


def _round_layer(w_all, layer, slab=256):
    _, rows, cols = w_all.shape
    return pl.pallas_call(
        _round_kernel,
        grid=(rows // slab,),
        in_specs=[pl.BlockSpec((None, slab, cols), lambda i: (layer, i, 0))],
        out_specs=pl.BlockSpec((slab, cols), lambda i: (i, 0)),
        out_shape=jax.ShapeDtypeStruct((rows, cols), BF16),
        compiler_params=_params(("parallel",), 0, claim_all=True),
        name="round_layer",
    )(w_all)


def kernel(x, positions, a_w_qkv, a_b_qkv, a_sinks, a_w_o, b_w_q, b_w_o, b_w_kv,
           ln_attn_g, ln_attn_b, ln_mlp_g, ln_mlp_b, mlp_w_in, mlp_w_out):
    batch, seq, d_model = x.shape
    assert batch == 1 and a_w_qkv.shape[0] == 1 and b_w_q.shape[0] == 1
    h = x.reshape(seq, d_model)
    pos_row = positions.reshape(1, seq)
    row = lambda v: v.reshape(1, -1)


    cos_a, sin_a = _rope_tables(pos_row, A_HEAD_DIM)
    n_a_heads = a_sinks.shape[1]
    qT, kn, vT, w_in0 = _project(
        h, [_round_layer(a_w_qkv, 0)], a_b_qkv, cos_a, sin_a,
        n_q_heads=n_a_heads, n_kv_heads=A_KV_HEADS, head_dim=A_HEAD_DIM,
        q_scale=A_HEAD_DIM ** -0.5 * LOG2_E, ride=[(mlp_w_in, 0)])
    sink_row = jnp.repeat(a_sinks[0].astype(F32), A_WINDOW)[None, :]
    o, w_out0, w_o0 = _swa_attention(
        qT, kn, vT, sink_row, n_kv_heads=A_KV_HEADS, head_dim=A_HEAD_DIM, blk=A_WINDOW,
        ride=[(mlp_w_out, 0), (a_w_o, 0)])
    h = _oproj_ln(o, w_o0, h, row(ln_attn_g[0]), row(ln_attn_b[0]))
    h, w_q1, w_kv1 = _mlp_ln(h, w_in0, w_out0, row(ln_mlp_g[0]), row(ln_mlp_b[0]),
                             ride=[(b_w_q, 0), (b_w_kv[None], 0)])

    cos_b, sin_b = _rope_tables(pos_row, B_HEAD_DIM)
    n_b_heads = b_w_q.shape[2] // B_HEAD_DIM
    qT, kn, vT, kbar, w_o1 = _project(
        h, [w_q1, w_kv1], None,
        cos_b, sin_b, n_q_heads=n_b_heads, n_kv_heads=B_KV_HEADS, head_dim=B_HEAD_DIM,
        q_scale=B_HEAD_DIM ** -0.5 * LOG2_E, kbar_block=MOBA_BLOCK, ride=[(b_w_o, 0)])
    kbar = kbar.reshape(seq // MOBA_BLOCK, B_KV_HEADS * B_HEAD_DIM)
    o, w_in1, w_out1 = _moba_attention(
        qT, kn, vT, kbar, n_kv_heads=B_KV_HEADS, head_dim=B_HEAD_DIM, blk=MOBA_BLOCK,
        topk=MOBA_TOPK, ride=[(mlp_w_in, 1), (mlp_w_out, 1)])
    h = _oproj_ln(o, w_o1, h, row(ln_attn_g[1]), row(ln_attn_b[1]))
    (h,) = _mlp_ln(h, w_in1, w_out1, row(ln_mlp_g[1]), row(ln_mlp_b[1]))
    return h.reshape(batch, seq, d_model)
```
